```python
import jax
import jax.numpy as jnp
from jax import lax
import numpy as np

D_MODEL = 1024
BATCH = 4
SEQ = 4096
DEPTH = 2

GRID_W = 64
CTX_LEN = 256
Q_BLOCK = 128
ROPE_THETA = 10000.0
NORM_EPS = 1e-6

GQA_HEADS = 8
GQA_KV_HEADS = 2
GQA_GROUP = GQA_HEADS // GQA_KV_HEADS
GQA_HEAD_DIM = 64
GQA_SCALE = GQA_HEAD_DIM ** -0.5

RWKV_HEADS = 8
RWKV_HEAD = 64
RWKV_DIM = RWKV_HEADS * RWKV_HEAD
DECAY_LORA = 64
AAA_LORA = 64
GATE_LORA = 128
LNX_EPS = 64e-5

MLA_HEADS = 8
Q_LORA = 384
KV_LORA = 256
QK_NOPE = 64
QK_ROPE = 32
V_HEAD = 64
MLA_SCALE = (QK_NOPE + QK_ROPE) ** -0.5

N_BRANCH = 3
BRANCH_W = 512
D_FF = 4 * D_MODEL

RWKV_SPLITS = (RWKV_DIM, RWKV_DIM, RWKV_DIM, 2 * DECAY_LORA, 2 * AAA_LORA, GATE_LORA)
RWKV_IN = 3 * RWKV_DIM + 2 * DECAY_LORA + 2 * AAA_LORA + GATE_LORA
SPLIT_SIZES = (GQA_HEADS * GQA_HEAD_DIM, GQA_KV_HEADS * GQA_HEAD_DIM, GQA_KV_HEADS * GQA_HEAD_DIM,
               RWKV_IN, Q_LORA, KV_LORA, QK_ROPE, N_BRANCH * D_MODEL)
N_IN = sum(SPLIT_SIZES)

kernel_name = 'hybrid_gqa_rwkv7_mla_dit_trunk'


def split_last(t, sizes):
    out, start = [], 0
    for s in sizes:
        out.append(t[..., start:start + s])
        start += s
    return out


def rms_norm(x, gain, eps=NORM_EPS):
    xf = x.astype(jnp.float32)
    y = xf * lax.rsqrt(jnp.mean(xf * xf, axis=-1, keepdims=True) + eps)
    return (y * gain.astype(jnp.float32)).astype(x.dtype)


def modulate(h, shift, scale):
    return h * (1.0 + scale) + shift


def rope_1d(x, pos):
    n = x.shape[-1] // 2
    inv_freq = ROPE_THETA ** (-jnp.arange(n, dtype=jnp.float32) / n)
    ang = pos.astype(jnp.float32)[:, None] * inv_freq[None, :]
    cos = jnp.cos(ang)[None, :, None, :]
    sin = jnp.sin(ang)[None, :, None, :]
    xf = x.astype(jnp.float32)
    x1, x2 = xf[..., :n], xf[..., n:]
    return jnp.concatenate([x1 * cos - x2 * sin, x1 * sin + x2 * cos], axis=-1).astype(x.dtype)


def axial_rope(x, rows, cols):
    half = x.shape[-1] // 2
    return jnp.concatenate([rope_1d(x[..., :half], rows), rope_1d(x[..., half:], cols)], axis=-1)


def block_attention(q, k, v, scale):
    b, t, hk, g, dq = q.shape
    nb = t // Q_BLOCK
    qb = jnp.moveaxis(q.reshape(b, nb, Q_BLOCK, hk, g, dq), 1, 0)

    def one_block(qi):
        s = jnp.einsum('bqhgd,bkhd->bhgqk', qi, k, preferred_element_type=jnp.float32) * scale
        p = jax.nn.softmax(s, axis=-1).astype(v.dtype)
        return jnp.einsum('bhgqk,bkhe->bqhge', p, v)

    o = lax.map(one_block, qb)
    return jnp.moveaxis(o, 0, 1).reshape(b, t, hk, g, v.shape[-1])


def gqa_branch(p_ctx, p_lat, q_gain, k_gain, pos, ctx_out):
    def heads(t, n):
        return t.reshape(t.shape[0], t.shape[1], n, GQA_HEAD_DIM)

    def keys_values(p, pos_):
        k = rms_norm(heads(p[1], GQA_KV_HEADS), k_gain)
        if pos_ is not None:
            k = axial_rope(k, *pos_)
        return k, heads(p[2], GQA_KV_HEADS)

    def queries(p, pos_):
        q = rms_norm(heads(p[0], GQA_HEADS), q_gain)
        if pos_ is not None:
            q = axial_rope(q, *pos_)
        return q.reshape(q.shape[0], q.shape[1], GQA_KV_HEADS, GQA_GROUP, GQA_HEAD_DIM)

    k_c, v_c = keys_values(p_ctx, None)
    k_l, v_l = keys_values(p_lat, pos)
    b, t = p_lat[0].shape[:2]
    y_lat = block_attention(queries(p_lat, pos), jnp.concatenate([k_c, k_l], axis=1),
                            jnp.concatenate([v_c, v_l], axis=1), GQA_SCALE).reshape(b, t, GQA_HEADS * GQA_HEAD_DIM)
    y_ctx = None
    if ctx_out:
        bc, tc = p_ctx[0].shape[:2]
        y_ctx = block_attention(queries(p_ctx, None), k_c, v_c, GQA_SCALE).reshape(bc, tc, GQA_HEADS * GQA_HEAD_DIM)
    return y_ctx, y_lat


def centred_shift(u, mu):
    zero = jnp.zeros_like(u[:, :1])
    prev = jnp.concatenate([zero, u[:, :-1]], axis=1)
    nxt = jnp.concatenate([u[:, 1:], zero], axis=1)
    return u + mu[0] * (prev - u) + mu[1] * (nxt - u)


def wkv7_scan(state0, r, decay, k, v, a_vec, b_vec, reverse, emit):
    def step(S, inp):
        r_t, w_t, k_t, v_t, a_t, b_t = inp
        sa = jnp.einsum('bhvk,bhk->bhv', S, a_t)
        S = S * w_t[:, :, None, :] + sa[..., None] * b_t[:, :, None, :] + v_t[..., None] * k_t[:, :, None, :]
        y = jnp.einsum('bhvk,bhk->bhv', S, r_t) if emit else None
        return S, y

    xs = tuple(jnp.moveaxis(t, 1, 0) for t in (r, decay, k, v, a_vec, b_vec))
    S, ys = lax.scan(step, state0, xs, reverse=reverse)
    return S, (jnp.moveaxis(ys, 0, 1) if emit else None)


def rwkv7_branch(u_ctx, u_lat, w0, w2, a0, a2, g2, k_k, k_a, r_k, ln_w, ln_b, ctx_out):
    out_dtype = u_lat.dtype
    k_a_h = k_a.reshape(RWKV_HEADS, RWKV_HEAD).astype(jnp.float32)

    def heads(t):
        return t.reshape(t.shape[0], t.shape[1], RWKV_HEADS, RWKV_HEAD)

    def prep(u):
        r, k, v, wd, ad, gd = split_last(u.astype(jnp.float32), RWKV_SPLITS)
        kk = heads(k * k_k)
        kk = kk / jnp.maximum(jnp.sqrt(jnp.sum(kk * kk, axis=-1, keepdims=True)), 1e-12)
        return heads(r), heads(k), heads(v), kk, wd, ad, gd

    def direction_inputs(k, kk, wd, ad, d):
        wd_d = wd[..., d * DECAY_LORA:(d + 1) * DECAY_LORA]
        ad_d = ad[..., d * AAA_LORA:(d + 1) * AAA_LORA]
        w_log = -jax.nn.softplus(-(w0[d] + jnp.tanh(wd_d) @ w2[d])) - 0.5
        decay = heads(jnp.exp(-jnp.exp(w_log)))
        a = heads(jax.nn.sigmoid(a0[d] + ad_d @ a2[d]))
        k_d = k * (1.0 + (a - 1.0) * k_a_h)
        return decay, k_d, -kk, kk * a

    def bonus(r, k_d, v):
        return jnp.sum(r * k_d * r_k, axis=-1, keepdims=True) * v

    def finish(y, bon, gd):
        b, t = y.shape[:2]
        mu = jnp.mean(y, axis=-1, keepdims=True)
        var = jnp.mean(jnp.square(y - mu), axis=-1, keepdims=True)
        yn = ((y - mu) * lax.rsqrt(var + LNX_EPS)).reshape(b, t, RWKV_DIM) * ln_w + ln_b
        out = (yn + bon.reshape(b, t, RWKV_DIM)) * (jax.nn.sigmoid(gd) @ g2)
        return out.astype(out_dtype)

    r_c, k_c, v_c, kk_c, wd_c, ad_c, gd_c = prep(u_ctx)
    r_l, k_l, v_l, kk_l, wd_l, ad_l, gd_l = prep(u_lat)
    state0 = jnp.zeros((u_ctx.shape[0], RWKV_HEADS, RWKV_HEAD, RWKV_HEAD), jnp.float32)
    ys_c, bs_c, ys_l, bs_l = [], [], [], []
    for d in range(2):
        rev = d == 1
        dec_c, kd_c, av_c, bv_c = direction_inputs(k_c, kk_c, wd_c, ad_c, d)
        s_ctx, y_c = wkv7_scan(state0, r_c, dec_c, kd_c, v_c, av_c, bv_c, rev, ctx_out)
        dec_l, kd_l, av_l, bv_l = direction_inputs(k_l, kk_l, wd_l, ad_l, d)
        _, y_l = wkv7_scan(s_ctx, r_l, dec_l, kd_l, v_l, av_l, bv_l, rev, True)
        ys_l.append(y_l)
        bs_l.append(bonus(r_l, kd_l, v_l))
        if ctx_out:
            ys_c.append(y_c)
            bs_c.append(bonus(r_c, kd_c, v_c))
    y_lat = finish(ys_l[0] + ys_l[1], bs_l[0] + bs_l[1], gd_l)
    y_ctx = finish(ys_c[0] + ys_c[1], bs_c[0] + bs_c[1], gd_c) if ctx_out else None
    return y_ctx, y_lat


def mla_branch(p_ctx, p_lat, q_norm, q_up, kv_norm, kv_up, pos, ctx_out):
    def keys_values(p, pos_):
        kv_down, k_rope = p[1], p[2]
        b, t = kv_down.shape[:2]
        kv = (rms_norm(kv_down, kv_norm) @ kv_up).reshape(b, t, MLA_HEADS, QK_NOPE + V_HEAD)
        k_rope = k_rope[:, :, None, :]
        if pos_ is not None:
            k_rope = axial_rope(k_rope, *pos_)
        k = jnp.concatenate([kv[..., :QK_NOPE], jnp.broadcast_to(k_rope, (b, t, MLA_HEADS, QK_ROPE))], axis=-1)
        return k, kv[..., QK_NOPE:]

    def queries(p, pos_):
        q_down = p[0]
        b, t = q_down.shape[:2]
        q = (rms_norm(q_down, q_norm) @ q_up).reshape(b, t, MLA_HEADS, QK_NOPE + QK_ROPE)
        if pos_ is not None:
            q = jnp.concatenate([q[..., :QK_NOPE], axial_rope(q[..., QK_NOPE:], *pos_)], axis=-1)
        return q[:, :, :, None, :]

    k_c, v_c = keys_values(p_ctx, None)
    k_l, v_l = keys_values(p_lat, pos)
    b, t = p_lat[0].shape[:2]
    y_lat = block_attention(queries(p_lat, pos), jnp.concatenate([k_c, k_l], axis=1),
                            jnp.concatenate([v_c, v_l], axis=1), MLA_SCALE).reshape(b, t, MLA_HEADS * V_HEAD)
    y_ctx = None
    if ctx_out:
        bc, tc = p_ctx[0].shape[:2]
        y_ctx = block_attention(queries(p_ctx, None), k_c, v_c, MLA_SCALE).reshape(bc, tc, MLA_HEADS * V_HEAD)
    return y_ctx, y_lat


def merge_branches(ys, gate_logits, w_branch, w_out):
    d = w_out.shape[0]
    gates = jax.nn.sigmoid(gate_logits)
    mixed = gates[..., :d] * (ys[0] @ w_branch[0])
    for i in range(1, N_BRANCH):
        mixed = mixed + gates[..., i * d:(i + 1) * d] * (ys[i] @ w_branch[i])
    return mixed @ w_out


def sqrelu_mlp(h, w1, w2):
    return jnp.square(jax.nn.relu(h @ w1)) @ w2


def setup_inputs(seed: int = 0) -> dict:
    key = jax.random.key(seed)
    keys = iter(jax.random.split(key, 48))
    f32 = jnp.float32
    L, D = DEPTH, D_MODEL

    def normal(shape, scale):
        return jax.random.normal(next(keys), shape, f32) * scale

    def uniform(shape, lo, hi):
        return jax.random.uniform(next(keys), shape, f32, lo, hi)

    return {
        'x': normal((BATCH, SEQ, D), 1.0),
        'c': normal((BATCH, D), 1.0),
        'ctx': normal((BATCH, CTX_LEN, D), 1.0),
        'c_ctx': normal((D,), 1.0),
        'w_mod': normal((L, D, 6 * D), 0.5 * D ** -0.5),
        'b_mod': normal((L, 6 * D), 0.02),
        'g_norm1': 1.0 + normal((L, D), 0.02),
        'g_norm2': 1.0 + normal((L, D), 0.02),
        'w_in': normal((L, D, N_IN), D ** -0.5),
        'gqa_q_gain': 1.0 + normal((L, GQA_HEAD_DIM), 0.02),
        'gqa_k_gain': 1.0 + normal((L, GQA_HEAD_DIM), 0.02),
        'rwkv_shift_mu': uniform((L, 2, RWKV_IN), 0.0, 0.5),
        'rwkv_w0': uniform((L, 2, RWKV_DIM), -6.0, -1.0),
        'rwkv_w2': normal((L, 2, DECAY_LORA, RWKV_DIM), 0.1 * DECAY_LORA ** -0.5),
        'rwkv_a0': normal((L, 2, RWKV_DIM), 0.1),
        'rwkv_a2': normal((L, 2, AAA_LORA, RWKV_DIM), 0.5 * AAA_LORA ** -0.5),
        'rwkv_g2': normal((L, GATE_LORA, RWKV_DIM), GATE_LORA ** -0.5),
        'rwkv_k_k': 0.85 + normal((L, RWKV_DIM), 0.02),
        'rwkv_k_a': 1.0 + normal((L, RWKV_DIM), 0.02),
        'rwkv_r_k': normal((L, RWKV_HEADS, RWKV_HEAD), 0.1),
        'rwkv_ln_w': 1.0 + normal((L, RWKV_DIM), 0.02),
        'rwkv_ln_b': normal((L, RWKV_DIM), 0.02),
        'mla_q_norm': 1.0 + normal((L, Q_LORA), 0.02),
        'mla_q_up': normal((L, Q_LORA, MLA_HEADS * (QK_NOPE + QK_ROPE)), Q_LORA ** -0.5),
        'mla_kv_norm': 1.0 + normal((L, KV_LORA), 0.02),
        'mla_kv_up': normal((L, KV_LORA, MLA_HEADS * (QK_NOPE + V_HEAD)), KV_LORA ** -0.5),
        'w_branch': normal((L, N_BRANCH, BRANCH_W, D), BRANCH_W ** -0.5),
        'w_out': normal((L, D, D), D ** -0.5),
        'w_ff1': normal((L, D, D_FF), D ** -0.5),
        'w_ff2': normal((L, D_FF, D), D_FF ** -0.5),
        'g_final': 1.0 + normal((D,), 0.02),
    }


def reference(x, c, ctx, c_ctx, w_mod, b_mod, g_norm1, g_norm2, w_in, gqa_q_gain, gqa_k_gain,
              rwkv_shift_mu, rwkv_w0, rwkv_w2, rwkv_a0, rwkv_a2, rwkv_g2, rwkv_k_k, rwkv_k_a, rwkv_r_k,
              rwkv_ln_w, rwkv_ln_b, mla_q_norm, mla_q_up, mla_kv_norm, mla_kv_up,
              w_branch, w_out, w_ff1, w_ff2, g_final):
    b, t, d = x.shape
    rows_n = t // GRID_W
    rows = jnp.repeat(jnp.arange(rows_n, dtype=jnp.int32), GRID_W)
    cols = jnp.tile(jnp.arange(GRID_W, dtype=jnp.int32), rows_n)
    pos = (rows, cols)

    for l in range(DEPTH):
        ctx_out = l < DEPTH - 1
        mod_lat = (jax.nn.silu(c) @ w_mod[l] + b_mod[l])[:, None, :]
        mod_ctx = (jax.nn.silu(c_ctx) @ w_mod[l] + b_mod[l])[None, None, :]
        sh_a, sc_a, gt_a, sh_f, sc_f, gt_f = split_last(mod_lat, (d,) * 6)
        csh_a, csc_a, cgt_a, csh_f, csc_f, cgt_f = split_last(mod_ctx, (d,) * 6)

        h_lat = modulate(rms_norm(x, g_norm1[l]), sh_a, sc_a)
        h_ctx = modulate(rms_norm(ctx, g_norm1[l]), csh_a, csc_a)
        p_lat = split_last(h_lat @ w_in[l], SPLIT_SIZES)
        p_ctx = split_last(h_ctx @ w_in[l], SPLIT_SIZES)

        a_ctx, a_lat = gqa_branch(p_ctx[0:3], p_lat[0:3], gqa_q_gain[l], gqa_k_gain[l], pos, ctx_out)
        r_ctx, r_lat = rwkv7_branch(centred_shift(p_ctx[3], rwkv_shift_mu[l]),
                                    centred_shift(p_lat[3], rwkv_shift_mu[l]),
                                    rwkv_w0[l], rwkv_w2[l], rwkv_a0[l], rwkv_a2[l], rwkv_g2[l],
                                    rwkv_k_k[l], rwkv_k_a[l], rwkv_r_k[l], rwkv_ln_w[l], rwkv_ln_b[l], ctx_out)
        m_ctx, m_lat = mla_branch(p_ctx[4:7], p_lat[4:7], mla_q_norm[l], mla_q_up[l],
                                  mla_kv_norm[l], mla_kv_up[l], pos, ctx_out)

        x = x + gt_a * merge_branches((a_lat, r_lat, m_lat), p_lat[7], w_branch[l], w_out[l])
        x = x + gt_f * sqrelu_mlp(modulate(rms_norm(x, g_norm2[l]), sh_f, sc_f), w_ff1[l], w_ff2[l])
        if ctx_out:
            ctx = ctx + cgt_a * merge_branches((a_ctx, r_ctx, m_ctx), p_ctx[7], w_branch[l], w_out[l])
            ctx = ctx + cgt_f * sqrelu_mlp(modulate(rms_norm(ctx, g_norm2[l]), csh_f, csc_f), w_ff1[l], w_ff2[l])

    return rms_norm(x, g_final)
```

```python
import functools

import jax
import jax.numpy as jnp
from jax import lax
from jax.experimental import pallas as pl
from jax.experimental.pallas import tpu as pltpu

F32 = jnp.float32
BF16 = jnp.bfloat16

D_MODEL = 1024
GRID_W = 64
ROPE_THETA = 10000.0
NORM_EPS = 1e-6

GQA_HEADS = 8
GQA_KV_HEADS = 2
GQA_HEAD_DIM = 64
GQA_SCALE = GQA_HEAD_DIM ** -0.5

RWKV_HEADS = 8
RWKV_HEAD = 64
RWKV_DIM = RWKV_HEADS * RWKV_HEAD
DECAY_LORA = 64
AAA_LORA = 64
GATE_LORA = 128
LNX_EPS = 64e-5
RWKV_IN = 3 * RWKV_DIM + 2 * DECAY_LORA + 2 * AAA_LORA + GATE_LORA

MLA_HEADS = 8
Q_LORA = 384
KV_LORA = 256
QK_NOPE = 64
QK_ROPE = 32
V_HEAD = 64
MLA_SCALE = (QK_NOPE + QK_ROPE) ** -0.5
MLA_SLOT = 128

N_BRANCH = 3
BRANCH_W = 512
D_FF = 4 * D_MODEL

LANES = 128
ROW_TILE = 256
CHUNK = 64
VMEM_LIMIT = 56 * 1024 * 1024

_C_GQ = 0
_C_GK = _C_GQ + GQA_HEADS * GQA_HEAD_DIM
_C_GV = _C_GK + GQA_KV_HEADS * GQA_HEAD_DIM
_C_RW = _C_GV + GQA_KV_HEADS * GQA_HEAD_DIM
_C_QD = _C_RW + RWKV_IN
_C_KVD = _C_QD + Q_LORA
_C_KR = _C_KVD + KV_LORA
_C_END = _C_KR + LANES


def _cparams(n_axes):
    return pltpu.CompilerParams(dimension_semantics=("arbitrary",) * n_axes,
                                vmem_limit_bytes=VMEM_LIMIT)


def _dot(a, b):
    return jnp.dot(a, b, preferred_element_type=F32)


def _dot_nt(a, b):
    return lax.dot_general(a, b, (((1,), (1,)), ((), ())), preferred_element_type=F32)


def _dot_tn(a, b):
    return lax.dot_general(a, b, (((0,), (0,)), ((), ())), preferred_element_type=F32)


def _split(x):
    hi = x.astype(BF16)
    lo = (x - hi.astype(F32)).astype(BF16)
    return hi, lo


def _dot3(a, b, dot=_dot):
    return dot(a[0], b[0]) + (dot(a[0], b[1]) + dot(a[1], b[0]))


def _group_sum(x, bd):
    outs = []
    for g in range(x.shape[1] // LANES):
        hi, lo = _split(x[:, g * LANES:(g + 1) * LANES])
        outs.append(_dot(hi, bd) + _dot(lo, bd))
    return outs[0] if len(outs) == 1 else jnp.concatenate(outs, axis=1)


def _rope(x, cos, sin, half):
    lane = lax.broadcasted_iota(jnp.int32, x.shape, 1)
    up = pltpu.roll(x, LANES - half, axis=1)
    dn = pltpu.roll(x, half, axis=1)
    sw = jnp.where((lane % (2 * half)) < half, up, dn)
    return x * cos + sw * sin


def _rms(x, gain):
    ms = jnp.mean(x * x, axis=-1, keepdims=True)
    return x * lax.rsqrt(ms + NORM_EPS) * gain


def _mod_kernel(c_ref, w_ref, b_ref, o_ref):
    c = c_ref[...]
    s = (c * jax.nn.sigmoid(c)).astype(BF16)
    o_ref[...] = _dot(s, w_ref[...].astype(BF16)) + b_ref[...]


def _mod_call(c_all, w_mod, b_mod):
    n_layers, d, n = w_mod.shape
    bn = 1536
    return pl.pallas_call(
        _mod_kernel,
        out_shape=jax.ShapeDtypeStruct((n_layers, 8, n), F32),
        grid=(n_layers, n // bn),
        in_specs=[
            pl.BlockSpec((8, d), lambda l, j: (0, 0)),
            pl.BlockSpec((None, d, bn), lambda l, j: (l, 0, j)),
            pl.BlockSpec((None, 1, bn), lambda l, j: (l, 0, j)),
        ],
        out_specs=pl.BlockSpec((None, 8, bn), lambda l, j: (l, 0, j)),
        compiler_params=_cparams(2),
        name="adaln_mod",
    )(c_all, w_mod, b_mod.reshape(n_layers, 1, n))


def _inproj_kernel(x_ref, g1_ref, sh_ref, sc_ref, w_ref, gain_ref, bd_ref,
                   gc_ref, gs_ref, mc_ref, ms_ref,
                   qn_ref, qup_ref, kvn_ref, kvk_ref, kvv_ref,
                   gq_ref, gk_ref, gv_ref, ru_ref, mq_ref, mk_ref, mv_ref):
    x = x_ref[...]
    h = _rms(x, g1_ref[...]) * (1.0 + sc_ref[...]) + sh_ref[...]
    p = _dot(h.astype(BF16), w_ref[...])

    bd = bd_ref[...]
    gc, gs = gc_ref[...], gs_ref[...]
    n_q_slabs = GQA_HEADS * GQA_HEAD_DIM // LANES
    n_k_slabs = GQA_KV_HEADS * GQA_HEAD_DIM // LANES
    for g in range(n_q_slabs + n_k_slabs):
        slab = p[:, g * LANES:(g + 1) * LANES]
        ms = _group_sum(slab * slab, bd) * (1.0 / GQA_HEAD_DIM)
        y = slab * lax.rsqrt(ms + NORM_EPS) * gain_ref[:, g * LANES:(g + 1) * LANES]
        y = _rope(y, gc, gs, GQA_HEAD_DIM // 4)
        if g < n_q_slabs:
            gq_ref[:, g * LANES:(g + 1) * LANES] = (y * GQA_SCALE).astype(BF16)
        else:
            gk_ref[:, (g - n_q_slabs) * LANES:(g - n_q_slabs + 1) * LANES] = y.astype(BF16)
    gv_ref[...] = p[:, _C_GV:_C_RW].astype(BF16)

    ru_ref[...] = p[:, _C_RW:_C_QD]

    mc, msn = mc_ref[...], ms_ref[...]
    qd = _rms(p[:, _C_QD:_C_KVD], qn_ref[...])
    q = _dot(qd.astype(BF16), qup_ref[...])
    kvd = _rms(p[:, _C_KVD:_C_KR], kvn_ref[...]).astype(BF16)
    kn = _dot(kvd, kvk_ref[...])
    kr = _rope(p[:, _C_KR:_C_END], mc, msn, QK_ROPE // 4)
    for hd in range(MLA_HEADS):
        sl = slice(hd * MLA_SLOT, (hd + 1) * MLA_SLOT)
        mq_ref[:, sl] = _rope(q[:, sl], mc, msn, QK_ROPE // 4).astype(BF16)
        mk_ref[:, sl] = (kn[:, sl] + kr).astype(BF16)
    mv_ref[...] = _dot(kvd, kvv_ref[...]).astype(BF16)


def _row_mod_spec(which, n_tiles, n_ctx_tiles, ctx_row):
    def index(i):
        b, t = i // n_tiles, i % n_tiles
        return (jnp.where(t < n_ctx_tiles, ctx_row, b), which, 0, 0)
    return pl.BlockSpec((None, None, 1, D_MODEL), index)


def _full(shape):
    return pl.BlockSpec(shape, lambda i: (0,) * len(shape))


def _rows(width):
    return pl.BlockSpec((ROW_TILE, width), lambda i: (i, 0))


def _inproj_call(xs, mod, g1, w_main, gains, bd, tabs, qn, qup, kvn, kvk, kvv, geom):
    n_rows, n_tiles, n_ctx_tiles, ctx_row = geom
    tab_spec = pl.BlockSpec((ROW_TILE, LANES), lambda i: (i % n_tiles, 0))
    widths = (GQA_HEADS * GQA_HEAD_DIM, GQA_KV_HEADS * GQA_HEAD_DIM, GQA_KV_HEADS * GQA_HEAD_DIM,
              RWKV_IN, MLA_HEADS * MLA_SLOT, MLA_HEADS * MLA_SLOT, MLA_HEADS * V_HEAD)
    dtypes = (BF16, BF16, BF16, F32, BF16, BF16, BF16)
    return pl.pallas_call(
        _inproj_kernel,
        out_shape=[jax.ShapeDtypeStruct((n_rows, w), dt) for w, dt in zip(widths, dtypes)],
        grid=(n_rows // ROW_TILE,),
        in_specs=[
            _rows(D_MODEL), _full((1, D_MODEL)),
            _row_mod_spec(0, n_tiles, n_ctx_tiles, ctx_row),
            _row_mod_spec(1, n_tiles, n_ctx_tiles, ctx_row),
            _full(w_main.shape), _full(gains.shape), _full(bd.shape),
            tab_spec, tab_spec, tab_spec, tab_spec,
            _full(qn.shape), _full(qup.shape), _full(kvn.shape), _full(kvk.shape), _full(kvv.shape),
        ],
        out_specs=[_rows(w) for w in widths],
        compiler_params=_cparams(1),
        name="in_proj",
    )(xs, g1, mod, mod, w_main, gains, bd, *tabs, qn, qup, kvn, kvk, kvv)


def _attn_kernel(q_ref, k_ref, v_ref, o_ref, *, heads, group, dq, dv, scale, n_ctx_tiles, ctx_len, s_tot):
    t = pl.program_id(1)

    def run(nk):
        for hq in range(heads):
            hk = hq // group
            q = q_ref[:, hq * dq:(hq + 1) * dq]
            k = k_ref[0:nk, hk * dq:(hk + 1) * dq]
            v = v_ref[0:nk, hk * dv:(hk + 1) * dv]
            s = _dot_nt(q, k)
            if scale is not None:
                s = s * scale
            m = jnp.max(s, axis=-1, keepdims=True)
            e = jnp.exp(s - m)
            l = jnp.sum(e, axis=-1, keepdims=True)
            o = _dot(e.astype(BF16), v) / l
            o_ref[:, hq * dv:(hq + 1) * dv] = o.astype(o_ref.dtype)

    @pl.when(t < n_ctx_tiles)
    def _():
        run(ctx_len)

    @pl.when(t >= n_ctx_tiles)
    def _():
        run(s_tot)


def _attn_call(q, k, v, *, heads, group, dq, dv, scale, batch, s_tot, ctx_len, name):
    n_tiles = s_tot // ROW_TILE
    kern = functools.partial(_attn_kernel, heads=heads, group=group, dq=dq, dv=dv, scale=scale,
                             n_ctx_tiles=ctx_len // ROW_TILE, ctx_len=ctx_len, s_tot=s_tot)
    return pl.pallas_call(
        kern,
        out_shape=jax.ShapeDtypeStruct((batch * s_tot, heads * dv), BF16),
        grid=(batch, n_tiles),
        in_specs=[
            pl.BlockSpec((ROW_TILE, q.shape[1]), lambda b, t: (b * n_tiles + t, 0)),
            pl.BlockSpec((s_tot, k.shape[1]), lambda b, t: (b, 0)),
            pl.BlockSpec((s_tot, v.shape[1]), lambda b, t: (b, 0)),
        ],
        out_specs=pl.BlockSpec((ROW_TILE, heads * dv), lambda b, t: (b * n_tiles + t, 0)),
        compiler_params=_cparams(2),
        name=name,
    )(q, k, v)


def _rwkv_prep_kernel(u_ref, prev_ref, next_ref, mu_ref, kk_ref, ka_ref, rk_ref, w0_ref, w2_ref,
                      a0_ref, a2_ref, g2_ref, bd_ref, tri_ref,
                      v_ref, bonus_ref, gate_ref, rt0_ref, at0_ref, kt0_ref, bt0_ref,
                      rt1_ref, at1_ref, kt1_ref, bt1_ref, pc_ref, *, n_tiles, n_ctx_tiles):
    t = pl.program_id(0) % n_tiles
    u = u_ref[...]
    rows = u.shape[0]
    row = lax.broadcasted_iota(jnp.int32, u.shape, 0)
    seq_start = jnp.logical_or(t == 0, t == n_ctx_tiles)
    seq_end = jnp.logical_or(t == n_ctx_tiles - 1, t == n_tiles - 1)
    halo_prev = prev_ref[7:8, :] * jnp.where(seq_start, 0.0, 1.0)
    halo_next = next_ref[0:1, :] * jnp.where(seq_end, 0.0, 1.0)
    prev = jnp.where(row == 0, halo_prev, pltpu.roll(u, 1, axis=0))
    nxt = jnp.where(row == rows - 1, halo_next, pltpu.roll(u, rows - 1, axis=0))
    us = u + mu_ref[0:1, :] * (prev - u) + mu_ref[1:2, :] * (nxt - u)

    c = RWKV_DIM
    r, k, v = us[:, 0:c], us[:, c:2 * c], us[:, 2 * c:3 * c]
    wd = jnp.tanh(us[:, 3 * c:3 * c + LANES]).astype(BF16)
    ad = us[:, 3 * c + LANES:3 * c + 2 * LANES].astype(BF16)
    gd = us[:, 3 * c + 2 * LANES:3 * c + 3 * LANES]
    bd = bd_ref[...]

    kk = k * kk_ref[...]
    kk = kk / jnp.maximum(jnp.sqrt(_group_sum(kk * kk, bd)), 1e-12)

    v_ref[...] = v
    gate_ref[...] = _dot(jax.nn.sigmoid(gd).astype(BF16), g2_ref[...])

    outs = ((rt0_ref, at0_ref, kt0_ref, bt0_ref), (rt1_ref, at1_ref, kt1_ref, bt1_ref))
    n_chunks = rows // CHUNK
    bonus = None
    for d in range(2):
        z = w0_ref[d:d + 1, :] + _dot(wd, w2_ref[d])
        nz = -z
        softplus = jnp.maximum(nz, 0.0) + jnp.log(1.0 + jnp.exp(-jnp.abs(nz)))
        lw = -jnp.exp(-softplus - 0.5)
        a = jax.nn.sigmoid(a0_ref[d:d + 1, :] + _dot(ad, a2_ref[d]))
        k_d = k * (1.0 + (a - 1.0) * ka_ref[...])
        b_d = _group_sum(r * k_d * rk_ref[...], bd) * v
        bonus = b_d if bonus is None else bonus + b_d

        tri = tri_ref[d]
        hi = lw.astype(BF16)
        rem = lw - hi.astype(F32)
        mid = rem.astype(BF16)
        lo = (rem - mid.astype(F32)).astype(BF16)
        cum = _dot(tri, hi) + (_dot(tri, mid) + _dot(tri, lo))
        e_neg = jnp.exp(-cum)
        rt_ref, at_ref, kt_ref, bt_ref = outs[d]
        rt_ref[...] = r * jnp.exp(cum)
        at_ref[...] = -kk * jnp.exp(cum - lw)
        kt_ref[...] = k_d * e_neg
        bt_ref[...] = kk * a * e_neg
        last = CHUNK - 1 if d == 0 else 0
        tot = jnp.concatenate([cum[ci * CHUNK + last:ci * CHUNK + last + 1, :] for ci in range(n_chunks)], axis=0)
        pc_ref[d * n_chunks:(d + 1) * n_chunks, :] = jnp.exp(tot)
    bonus_ref[...] = bonus


def _rwkv_prep_call(ru, mu, kk, ka, rk, w0, w2p, a0, a2p, g2, bd, tri, geom):
    n_rows, n_tiles, n_ctx_tiles, _ = geom
    grid = n_rows // ROW_TILE
    g8 = ROW_TILE // 8
    last8 = n_rows // 8 - 1
    kern = functools.partial(_rwkv_prep_kernel, n_tiles=n_tiles, n_ctx_tiles=n_ctx_tiles)
    c = RWKV_DIM
    n_pc = 2 * (ROW_TILE // CHUNK)
    return pl.pallas_call(
        kern,
        out_shape=[jax.ShapeDtypeStruct((n_rows, c), F32)] * 11
                  + [jax.ShapeDtypeStruct((grid * n_pc, c), F32)],
        grid=(grid,),
        in_specs=[
            _rows(RWKV_IN),
            pl.BlockSpec((8, RWKV_IN), lambda i: (jnp.maximum(i * g8 - 1, 0), 0)),
            pl.BlockSpec((8, RWKV_IN), lambda i: (jnp.minimum((i + 1) * g8, last8), 0)),
            _full(mu.shape), _full(kk.shape), _full(ka.shape), _full(rk.shape),
            _full(w0.shape), _full(w2p.shape), _full(a0.shape), _full(a2p.shape),
            _full(g2.shape), _full(bd.shape), _full(tri.shape),
        ],
        out_specs=[_rows(c)] * 11 + [pl.BlockSpec((n_pc, c), lambda i: (i, 0))],
        compiler_params=_cparams(1),
        name="rwkv_prep",
    )(ru, ru, ru, mu, kk, ka, rk, w0, w2p, a0, a2p, g2, bd, tri)


def _rwkv_scan_kernel(*refs):
    ins, (y0_ref, y1_ref, s_ref) = refs[:12], refs[12:]
    y_refs = (y0_ref, y1_ref)

    @pl.when(pl.program_id(1) == 0)
    def _():
        s_ref[...] = jnp.zeros_like(s_ref)

    n = RWKV_HEAD
    ri = lax.broadcasted_iota(jnp.int32, (2 * CHUNK, 2 * CHUNK), 0)
    ci = lax.broadcasted_iota(jnp.int32, (2 * CHUNK, 2 * CHUNK), 1)
    tq, ts = ri % CHUNK, ci % CHUNK
    eye = (lax.broadcasted_iota(jnp.int32, (CHUNK, CHUNK), 0)
           == lax.broadcasted_iota(jnp.int32, (CHUNK, CHUNK), 1)).astype(F32)

    for d in range(2):
        rt_ref, at_ref, kt_ref, bt_ref, v_ref, pc_ref = ins[6 * d:6 * d + 6]
        earlier = (ts < tq) if d == 0 else (ts > tq)
        mask = jnp.where(jnp.logical_or(earlier, jnp.logical_and(ri >= CHUNK, ts == tq)), 1.0, 0.0)
        lhs = jnp.concatenate([at_ref[...], rt_ref[...]], axis=0)
        rhs = jnp.concatenate([bt_ref[...], kt_ref[...]], axis=0)
        vv = v_ref[...]
        pc = pc_ref[...]
        s_all = s_ref[d]
        for hd in range(RWKV_HEADS):
            sl = slice(hd * n, (hd + 1) * n)
            lhs_h, rhs_h = _split(lhs[:, sl]), _split(rhs[:, sl])
            gram = _dot3(lhs_h, rhs_h, _dot_nt) * mask
            a_ab = gram[0:CHUNK, 0:CHUNK]
            a_ak = _split(gram[0:CHUNK, CHUNK:])
            a_r = _split(gram[CHUNK:, :])
            x = a_ab
            tm = eye + x
            for _ in range(CHUNK.bit_length() - 2):
                xs_ = _split(x)
                x = _dot3(xs_, xs_)
                tm = tm + _dot3(_split(tm), _split(x))
            s_h = s_all[:, sl]
            s_sp = _split(s_h)
            v_h = vv[:, sl]
            v_sp = _split(v_h)
            at_h = (lhs_h[0][0:CHUNK], lhs_h[1][0:CHUNK])
            rt_h = (lhs_h[0][CHUNK:], lhs_h[1][CHUNK:])
            xin = _dot3(at_h, s_sp, _dot_nt) + _dot3(a_ak, v_sp)
            u = _dot3(_split(tm), _split(xin))
            uv = jnp.concatenate([u, v_h], axis=0)
            y = _dot3(rt_h, s_sp, _dot_nt) + _dot3(a_r, _split(uv))
            y_refs[d][:, sl] = y
            s_new = (s_h + _dot3(_split(uv), rhs_h, _dot_tn)) * pc[:, sl]
            s_ref[d, :, sl] = s_new


def _rwkv_scan_call(prep, batch, s_tot, ctx_len):
    v, _, _, rt0, at0, kt0, bt0, rt1, at1, kt1, bt1, pc = prep
    n_ch = s_tot // CHUNK
    n_ctx_ch = ctx_len // CHUNK
    per_tile = ROW_TILE // CHUNK
    c = RWKV_DIM

    def chunk_of(d, i):
        if d == 0:
            return i
        return jnp.where(i < n_ctx_ch, n_ctx_ch - 1 - i, n_ctx_ch + n_ch - 1 - i)

    def row_spec(d):
        return pl.BlockSpec((CHUNK, c), lambda b, i: (b * n_ch + chunk_of(d, i), 0))

    def pc_spec(d):
        def index(b, i):
            ch = b * n_ch + chunk_of(d, i)
            return ((ch // per_tile) * 2 * per_tile + d * per_tile + ch % per_tile, 0, 0)
        return pl.BlockSpec((None, 1, c), index)

    pc3 = pc.reshape(pc.shape[0], 1, c)
    ins, specs = [], []
    for d, arrs in enumerate(((rt0, at0, kt0, bt0), (rt1, at1, kt1, bt1))):
        ins += list(arrs) + [v, pc3]
        specs += [row_spec(d)] * 5 + [pc_spec(d)]
    return pl.pallas_call(
        _rwkv_scan_kernel,
        out_shape=[jax.ShapeDtypeStruct((batch * s_tot, c), F32)] * 2,
        grid=(batch, n_ch),
        in_specs=specs,
        out_specs=[row_spec(0), row_spec(1)],
        scratch_shapes=[pltpu.VMEM((2, RWKV_HEAD, c), F32)],
        compiler_params=_cparams(2),
        name="rwkv_scan",
    )(*ins)


def _merge_kernel(x_ref, g1_ref, sh_ref, sc_ref, gt_ref, a_ref, y0_ref, y1_ref, bonus_ref, rg_ref, m_ref,
                  lnw_ref, lnb_ref, bd_ref, wg_ref, wb_ref, wo_ref, o_ref):
    x = x_ref[...]
    h = (_rms(x, g1_ref[...]) * (1.0 + sc_ref[...]) + sh_ref[...]).astype(BF16)

    bd = bd_ref[...]
    y = y0_ref[...] + y1_ref[...]
    mu = _group_sum(y, bd) * (1.0 / RWKV_HEAD)
    yc = y - mu
    var = _group_sum(yc * yc, bd) * (1.0 / RWKV_HEAD)
    yn = yc * lax.rsqrt(var + LNX_EPS) * lnw_ref[...] + lnb_ref[...]
    r_out = ((yn + bonus_ref[...]) * rg_ref[...]).astype(BF16)

    d = D_MODEL
    mixed = None
    for i, yb in enumerate((a_ref[...], r_out, m_ref[...])):
        gate = jax.nn.sigmoid(_dot(h, wg_ref[:, i * d:(i + 1) * d]))
        term = gate * _dot(yb, wb_ref[i])
        mixed = term if mixed is None else mixed + term
    o_ref[...] = x + gt_ref[...] * _dot(mixed.astype(BF16), wo_ref[...])


def _merge_call(xs, mod, g1, a, y0, y1, bonus, rg, m, lnw, lnb, bd, wg, wb, wo, geom):
    n_rows, n_tiles, n_ctx_tiles, ctx_row = geom
    ms = lambda which: _row_mod_spec(which, n_tiles, n_ctx_tiles, ctx_row)
    c = RWKV_DIM
    return pl.pallas_call(
        _merge_kernel,
        out_shape=jax.ShapeDtypeStruct((n_rows, D_MODEL), F32),
        grid=(n_rows // ROW_TILE,),
        in_specs=[
            _rows(D_MODEL), _full((1, D_MODEL)), ms(0), ms(1), ms(2),
            _rows(c), _rows(c), _rows(c), _rows(c), _rows(c), _rows(c),
            _full(lnw.shape), _full(lnb.shape), _full(bd.shape),
            _full(wg.shape), _full(wb.shape), _full(wo.shape),
        ],
        out_specs=_rows(D_MODEL),
        compiler_params=_cparams(1),
        name="merge",
    )(xs, g1, mod, mod, mod, a, y0, y1, bonus, rg, m, lnw, lnb, bd, wg, wb, wo)


def _mlp_kernel(x_ref, g2_ref, sh_ref, sc_ref, gt_ref, w1_ref, w2_ref, gf_ref, o_ref, *, final_norm):
    x = x_ref[...]
    h = (_rms(x, g2_ref[...]) * (1.0 + sc_ref[...]) + sh_ref[...]).astype(BF16)
    acc = None
    blk = D_MODEL
    for j in range(D_FF // blk):
        a = jnp.maximum(_dot(h, w1_ref[:, j * blk:(j + 1) * blk]), 0.0)
        part = _dot((a * a).astype(BF16), w2_ref[j * blk:(j + 1) * blk, :])
        acc = part if acc is None else acc + part
    out = x + gt_ref[...] * acc
    if final_norm:
        out = _rms(out, gf_ref[...])
    o_ref[...] = out


def _mlp_call(xs, mod, g2, w1, w2, gf, geom, final_norm):
    n_rows, n_tiles, n_ctx_tiles, ctx_row = geom
    ms = lambda which: _row_mod_spec(which, n_tiles, n_ctx_tiles, ctx_row)
    return pl.pallas_call(
        functools.partial(_mlp_kernel, final_norm=final_norm),
        out_shape=jax.ShapeDtypeStruct((n_rows, D_MODEL), F32),
        grid=(n_rows // ROW_TILE,),
        in_specs=[
            _rows(D_MODEL), _full((1, D_MODEL)), ms(3), ms(4), ms(5),
            _full(w1.shape), _full(w2.shape), _full((1, D_MODEL)),
        ],
        out_specs=_rows(D_MODEL),
        compiler_params=_cparams(1),
        name="mlp",
    )(xs, g2, mod, mod, mod, w1, w2, gf)


def _rope_tables(ctx_len, seq):
    pos = jnp.arange(seq, dtype=jnp.int32)
    rows, cols = pos // GRID_W, pos % GRID_W

    def cs(p, n):
        inv = ROPE_THETA ** (-jnp.arange(n, dtype=F32) / n)
        ang = p.astype(F32)[:, None] * inv[None, :]
        return jnp.cos(ang), jnp.sin(ang)

    def axial(n):
        cr, sr = cs(rows, n)
        cc, sc = cs(cols, n)
        return jnp.concatenate([cr, cr, cc, cc], -1), jnp.concatenate([-sr, sr, -sc, sc], -1)

    def with_ctx(cos, sin):
        w = cos.shape[1]
        return (jnp.concatenate([jnp.ones((ctx_len, w), F32), cos], 0),
                jnp.concatenate([jnp.zeros((ctx_len, w), F32), sin], 0))

    gc, gs = axial(GQA_HEAD_DIM // 4)
    gc, gs = jnp.tile(gc, (1, 2)), jnp.tile(gs, (1, 2))
    mc, ms = axial(QK_ROPE // 4)
    pad = MLA_SLOT - QK_NOPE - QK_ROPE
    mc = jnp.concatenate([jnp.ones((seq, QK_NOPE), F32), mc, jnp.ones((seq, pad), F32)], -1)
    ms = jnp.concatenate([jnp.zeros((seq, QK_NOPE), F32), ms, jnp.zeros((seq, pad), F32)], -1)
    return with_ctx(gc, gs) + with_ctx(mc, ms)


def _block_ones(n, blk):
    i = jnp.arange(n)
    return (i[:, None] // blk == i[None, :] // blk)


def _chunk_tri():
    i = jnp.arange(ROW_TILE)
    same = _block_ones(ROW_TILE, CHUNK)
    fwd = jnp.logical_and(same, i[None, :] <= i[:, None])
    bwd = jnp.logical_and(same, i[None, :] >= i[:, None])
    return jnp.stack([fwd, bwd]).astype(BF16)


def _pack_layer(l, w_in, gqa_q_gain, gqa_k_gain, mla_q_up, mla_kv_up, rwkv_w2, rwkv_a2):
    w = w_in[l]
    o_q = 0
    o_rw = GQA_HEADS * GQA_HEAD_DIM + 2 * GQA_KV_HEADS * GQA_HEAD_DIM
    o_qd = o_rw + RWKV_IN
    o_kr = o_qd + Q_LORA + KV_LORA
    o_gate = o_kr + QK_ROPE
    zeros = lambda n: jnp.zeros((D_MODEL, n), F32)
    w_main = jnp.concatenate(
        [w[:, o_q:o_kr], zeros(QK_NOPE), w[:, o_kr:o_gate], zeros(MLA_SLOT - QK_NOPE - QK_ROPE)], axis=1)
    w_gate = w[:, o_gate:]
    gains = jnp.concatenate([jnp.tile(gqa_q_gain[l], GQA_HEADS), jnp.tile(gqa_k_gain[l], GQA_KV_HEADS)])[None, :]

    qup = mla_q_up[l].reshape(Q_LORA, MLA_HEADS, QK_NOPE + QK_ROPE)
    qup = jnp.pad(qup, ((0, 0), (0, 0), (0, MLA_SLOT - QK_NOPE - QK_ROPE))).reshape(Q_LORA, MLA_HEADS * MLA_SLOT)
    kvu = mla_kv_up[l].reshape(KV_LORA, MLA_HEADS, QK_NOPE + V_HEAD)
    kvk = jnp.pad(kvu[:, :, :QK_NOPE], ((0, 0), (0, 0), (0, MLA_SLOT - QK_NOPE))).reshape(KV_LORA, MLA_HEADS * MLA_SLOT)
    kvv = kvu[:, :, QK_NOPE:].reshape(KV_LORA, MLA_HEADS * V_HEAD)

    def pad_dir(w2):
        lora = w2.shape[1]
        z = jnp.zeros_like(w2[0])
        return jnp.stack([jnp.concatenate([w2[0], z], 0), jnp.concatenate([z, w2[1]], 0)]).astype(BF16)

    return dict(w_main=w_main.astype(BF16), w_gate=w_gate.astype(BF16), gains=gains,
                qup=qup.astype(BF16), kvk=kvk.astype(BF16), kvv=kvv.astype(BF16),
                w2p=pad_dir(rwkv_w2[l]), a2p=pad_dir(rwkv_a2[l]))


def kernel(x, c, ctx, c_ctx, w_mod, b_mod, g_norm1, g_norm2, w_in, gqa_q_gain, gqa_k_gain, rwkv_shift_mu, rwkv_w0, rwkv_w2, rwkv_a0, rwkv_a2, rwkv_g2, rwkv_k_k, rwkv_k_a, rwkv_r_k, rwkv_ln_w, rwkv_ln_b, mla_q_norm, mla_q_up, mla_kv_norm, mla_kv_up, w_branch, w_out, w_ff1, w_ff2, g_final):
    batch, seq, d = x.shape
    ctx_len = ctx.shape[1]
    depth = w_mod.shape[0]
    s_tot = ctx_len + seq
    assert d == D_MODEL and batch < 8
    assert ctx_len % ROW_TILE == 0 and seq % ROW_TILE == 0 and seq % GRID_W == 0
    n_rows = batch * s_tot
    geom = (n_rows, s_tot // ROW_TILE, ctx_len // ROW_TILE, batch)

    c_all = jnp.concatenate([c, c_ctx[None, :], jnp.zeros((8 - batch - 1, d), F32)], axis=0)
    mod_all = _mod_call(c_all, w_mod, b_mod).reshape(depth, 8, 6, 1, d)

    tabs = _rope_tables(ctx_len, seq)
    bd = _block_ones(LANES, RWKV_HEAD).astype(BF16)
    tri = _chunk_tri()
    row1 = lambda v: v.reshape(1, -1)

    xs = jnp.concatenate([ctx, x], axis=1).reshape(n_rows, d)
    for l in range(depth):
        pk = _pack_layer(l, w_in, gqa_q_gain, gqa_k_gain, mla_q_up, mla_kv_up, rwkv_w2, rwkv_a2)
        mod = mod_all[l]
        g1 = row1(g_norm1[l])
        gq, gk, gv, ru, mq, mk, mv = _inproj_call(
            xs, mod, g1, pk["w_main"], pk["gains"], bd, tabs,
            row1(mla_q_norm[l]), pk["qup"], row1(mla_kv_norm[l]), pk["kvk"], pk["kvv"], geom)
        a_out = _attn_call(gq, gk, gv, heads=GQA_HEADS, group=GQA_HEADS // GQA_KV_HEADS, dq=GQA_HEAD_DIM,
                           dv=GQA_HEAD_DIM, scale=None, batch=batch, s_tot=s_tot, ctx_len=ctx_len, name="gqa_attn")
        m_out = _attn_call(mq, mk, mv, heads=MLA_HEADS, group=1, dq=MLA_SLOT, dv=V_HEAD, scale=MLA_SCALE,
                           batch=batch, s_tot=s_tot, ctx_len=ctx_len, name="mla_attn")
        prep = _rwkv_prep_call(ru, rwkv_shift_mu[l], row1(rwkv_k_k[l]), row1(rwkv_k_a[l]), row1(rwkv_r_k[l]),
                               rwkv_w0[l], pk["w2p"], rwkv_a0[l], pk["a2p"], rwkv_g2[l].astype(BF16), bd, tri, geom)
        y0, y1 = _rwkv_scan_call(prep, batch, s_tot, ctx_len)
        xs = _merge_call(xs, mod, g1, a_out, y0, y1, prep[1], prep[2], m_out,
                         row1(rwkv_ln_w[l]), row1(rwkv_ln_b[l]), bd, pk["w_gate"],
                         w_branch[l].astype(BF16), w_out[l].astype(BF16), geom)
        xs = _mlp_call(xs, mod, row1(g_norm2[l]), w_ff1[l].astype(BF16), w_ff2[l].astype(BF16),
                       row1(g_final), geom, final_norm=(l == depth - 1))
    return xs.reshape(batch, s_tot, d)[:, ctx_len:, :]
```

```python
import functools

import jax
import jax.numpy as jnp
from jax import lax
from jax.experimental import pallas as pl
from jax.experimental.pallas import tpu as pltpu

F32 = jnp.float32
BF16 = jnp.bfloat16

D_MODEL = 1024
GRID_W = 64
ROPE_THETA = 10000.0
NORM_EPS = 1e-6

GQA_HEADS = 8
GQA_KV_HEADS = 2
GQA_HEAD_DIM = 64
GQA_SCALE = GQA_HEAD_DIM ** -0.5

RWKV_HEADS = 8
RWKV_HEAD = 64
RWKV_DIM = RWKV_HEADS * RWKV_HEAD
DECAY_LORA = 64
AAA_LORA = 64
GATE_LORA = 128
LNX_EPS = 64e-5
RWKV_IN = 3 * RWKV_DIM + 2 * DECAY_LORA + 2 * AAA_LORA + GATE_LORA

MLA_HEADS = 8
Q_LORA = 384
KV_LORA = 256
QK_NOPE = 64
QK_ROPE = 32
V_HEAD = 64
MLA_SCALE = (QK_NOPE + QK_ROPE) ** -0.5
MLA_SLOT = 128
V_SLOT = 128
LOG2_E = 1.4426950408889634

N_BRANCH = 3
BRANCH_W = 512
D_FF = 4 * D_MODEL

LANES = 128
ROW_TILE = 256
CHUNK = 64
VMEM_LIMIT = 56 * 1024 * 1024

_C_GQ = 0
_C_GK = _C_GQ + GQA_HEADS * GQA_HEAD_DIM
_C_GV = _C_GK + GQA_KV_HEADS * GQA_HEAD_DIM
_C_RW = _C_GV + GQA_KV_HEADS * V_SLOT
_C_QD = _C_RW + RWKV_IN
_C_KVD = _C_QD + Q_LORA
_C_KR = _C_KVD + KV_LORA
_C_END = _C_KR + LANES


def _cparams(n_axes):
    return pltpu.CompilerParams(dimension_semantics=("arbitrary",) * n_axes,
                                vmem_limit_bytes=VMEM_LIMIT)


def _dot(a, b):
    return jnp.dot(a, b, preferred_element_type=F32)


def _dot_nt(a, b):
    return lax.dot_general(a, b, (((1,), (1,)), ((), ())), preferred_element_type=F32)


def _dot_tn(a, b):
    return lax.dot_general(a, b, (((0,), (0,)), ((), ())), preferred_element_type=F32)


def _split(x):
    hi = x.astype(BF16)
    lo = (x - hi.astype(F32)).astype(BF16)
    return hi, lo


def _dot3(a, b, dot=_dot):
    return dot(a[0], b[0]) + (dot(a[0], b[1]) + dot(a[1], b[0]))


def _group_sum(x, bd):
    outs = []
    for g in range(x.shape[1] // LANES):
        hi, lo = _split(x[:, g * LANES:(g + 1) * LANES])
        outs.append(_dot(hi, bd) + _dot(lo, bd))
    return outs[0] if len(outs) == 1 else jnp.concatenate(outs, axis=1)


def _rope(x, cos, sin, half):
    lane = lax.broadcasted_iota(jnp.int32, x.shape, 1)
    up = pltpu.roll(x, LANES - half, axis=1)
    dn = pltpu.roll(x, half, axis=1)
    sw = jnp.where((lane % (2 * half)) < half, up, dn)
    return x * cos + sw * sin


def _with_ones_lane(v):
    lane = lax.broadcasted_iota(jnp.int32, v.shape, 1)
    return jnp.where(lane % V_SLOT == V_HEAD, 1.0, v)


def _rms(x, gain):
    ms = jnp.mean(x * x, axis=-1, keepdims=True)
    return x * lax.rsqrt(ms + NORM_EPS) * gain


def _mod_kernel(c_ref, w_ref, b_ref, o_ref):
    c = c_ref[...]
    s = (c * jax.nn.sigmoid(c)).astype(BF16)
    o_ref[...] = _dot(s, w_ref[...].astype(BF16)) + b_ref[...]


def _mod_call(c_all, w_mod, b_mod):
    n_layers, d, n = w_mod.shape
    bn = 1536
    return pl.pallas_call(
        _mod_kernel,
        out_shape=jax.ShapeDtypeStruct((n_layers, 8, n), F32),
        grid=(n_layers, n // bn),
        in_specs=[
            pl.BlockSpec((8, d), lambda l, j: (0, 0)),
            pl.BlockSpec((None, d, bn), lambda l, j: (l, 0, j)),
            pl.BlockSpec((None, 1, bn), lambda l, j: (l, 0, j)),
        ],
        out_specs=pl.BlockSpec((None, 8, bn), lambda l, j: (l, 0, j)),
        compiler_params=_cparams(2),
        name="adaln_mod",
    )(c_all, w_mod, b_mod.reshape(n_layers, 1, n))


def _inproj_kernel(x_ref, g1_ref, sh_ref, sc_ref, w_ref, gain_ref, bd_ref,
                   gc_ref, gs_ref, mc_ref, ms_ref,
                   qn_ref, qup_ref, kvn_ref, kvk_ref, kvv_ref,
                   gq_ref, gk_ref, gv_ref, ru_ref, mq_ref, mk_ref, mv_ref):
    x = x_ref[...]
    h = _rms(x, g1_ref[...]) * (1.0 + sc_ref[...]) + sh_ref[...]
    p = _dot(h.astype(BF16), w_ref[...])

    bd = bd_ref[...]
    gc, gs = gc_ref[...], gs_ref[...]
    n_q_slabs = GQA_HEADS * GQA_HEAD_DIM // LANES
    n_k_slabs = GQA_KV_HEADS * GQA_HEAD_DIM // LANES
    for g in range(n_q_slabs + n_k_slabs):
        slab = p[:, g * LANES:(g + 1) * LANES]
        ms = _group_sum(slab * slab, bd) * (1.0 / GQA_HEAD_DIM)
        y = slab * lax.rsqrt(ms + NORM_EPS) * gain_ref[:, g * LANES:(g + 1) * LANES]
        y = _rope(y, gc, gs, GQA_HEAD_DIM // 4)
        if g < n_q_slabs:
            gq_ref[:, g * LANES:(g + 1) * LANES] = (y * GQA_SCALE).astype(BF16)
        else:
            gk_ref[:, (g - n_q_slabs) * LANES:(g - n_q_slabs + 1) * LANES] = y.astype(BF16)
    gv_ref[...] = _with_ones_lane(p[:, _C_GV:_C_RW]).astype(BF16)

    ru_ref[...] = p[:, _C_RW:_C_QD]

    mc, msn = mc_ref[...], ms_ref[...]
    qd = _rms(p[:, _C_QD:_C_KVD], qn_ref[...])
    q = _dot(qd.astype(BF16), qup_ref[...])
    kvd = _rms(p[:, _C_KVD:_C_KR], kvn_ref[...]).astype(BF16)
    kn = _dot(kvd, kvk_ref[...])
    kr = _rope(p[:, _C_KR:_C_END], mc, msn, QK_ROPE // 4)
    for hd in range(MLA_HEADS):
        sl = slice(hd * MLA_SLOT, (hd + 1) * MLA_SLOT)
        mq_ref[:, sl] = _rope(q[:, sl], mc, msn, QK_ROPE // 4).astype(BF16)
        mk_ref[:, sl] = (kn[:, sl] + kr).astype(BF16)
    mv_ref[...] = _with_ones_lane(_dot(kvd, kvv_ref[...])).astype(BF16)


def _row_mod_spec(which, n_tiles, n_ctx_tiles, ctx_row):
    def index(i):
        b, t = i // n_tiles, i % n_tiles
        return (jnp.where(t < n_ctx_tiles, ctx_row, b), which, 0, 0)
    return pl.BlockSpec((None, None, 1, D_MODEL), index)


def _full(shape):
    return pl.BlockSpec(shape, lambda i: (0,) * len(shape))


def _rows(width):
    return pl.BlockSpec((ROW_TILE, width), lambda i: (i, 0))


def _inproj_call(xs, mod, g1, w_main, gains, bd, tabs, qn, qup, kvn, kvk, kvv, geom):
    n_rows, n_tiles, n_ctx_tiles, ctx_row = geom
    tab_spec = pl.BlockSpec((ROW_TILE, LANES), lambda i: (i % n_tiles, 0))
    widths = (GQA_HEADS * GQA_HEAD_DIM, GQA_KV_HEADS * GQA_HEAD_DIM, GQA_KV_HEADS * V_SLOT,
              RWKV_IN, MLA_HEADS * MLA_SLOT, MLA_HEADS * MLA_SLOT, MLA_HEADS * V_SLOT)
    dtypes = (BF16, BF16, BF16, F32, BF16, BF16, BF16)
    return pl.pallas_call(
        _inproj_kernel,
        out_shape=[jax.ShapeDtypeStruct((n_rows, w), dt) for w, dt in zip(widths, dtypes)],
        grid=(n_rows // ROW_TILE,),
        in_specs=[
            _rows(D_MODEL), _full((1, D_MODEL)),
            _row_mod_spec(0, n_tiles, n_ctx_tiles, ctx_row),
            _row_mod_spec(1, n_tiles, n_ctx_tiles, ctx_row),
            _full(w_main.shape), _full(gains.shape), _full(bd.shape),
            tab_spec, tab_spec, tab_spec, tab_spec,
            _full(qn.shape), _full(qup.shape), _full(kvn.shape), _full(kvk.shape), _full(kvv.shape),
        ],
        out_specs=[_rows(w) for w in widths],
        compiler_params=_cparams(1),
        name="in_proj",
    )(xs, g1, mod, mod, w_main, gains, bd, *tabs, qn, qup, kvn, kvk, kvv)


def _attn_kernel(q_ref, k_ref, v_ref, o_ref, *, heads, group, dq, dv, scale, n_ctx_tiles, ctx_len, s_tot):
    t = pl.program_id(1)

    def run(nk):
        for hq in range(heads):
            hk = hq // group
            q = q_ref[:, hq * dq:(hq + 1) * dq]
            k = k_ref[0:nk, hk * dq:(hk + 1) * dq]
            v = v_ref[0:nk, hk * V_SLOT:(hk + 1) * V_SLOT]
            s = _dot_nt(q, k) * (scale * LOG2_E)
            m = jnp.max(s, axis=-1, keepdims=True)
            e = jnp.exp2(s - m).astype(BF16)
            ov = _dot(e, v)
            o_ref[:, hq * dv:(hq + 1) * dv] = (ov[:, 0:dv] / ov[:, dv:dv + 1]).astype(o_ref.dtype)

    @pl.when(t < n_ctx_tiles)
    def _():
        run(ctx_len)

    @pl.when(t >= n_ctx_tiles)
    def _():
        run(s_tot)


def _attn_call(q, k, v, *, heads, group, dq, dv, scale, batch, s_tot, ctx_len, name):
    n_tiles = s_tot // ROW_TILE
    kern = functools.partial(_attn_kernel, heads=heads, group=group, dq=dq, dv=dv, scale=scale,
                             n_ctx_tiles=ctx_len // ROW_TILE, ctx_len=ctx_len, s_tot=s_tot)
    return pl.pallas_call(
        kern,
        out_shape=jax.ShapeDtypeStruct((batch * s_tot, heads * dv), BF16),
        grid=(batch, n_tiles),
        in_specs=[
            pl.BlockSpec((ROW_TILE, q.shape[1]), lambda b, t: (b * n_tiles + t, 0)),
            pl.BlockSpec((s_tot, k.shape[1]), lambda b, t: (b, 0)),
            pl.BlockSpec((s_tot, v.shape[1]), lambda b, t: (b, 0)),
        ],
        out_specs=pl.BlockSpec((ROW_TILE, heads * dv), lambda b, t: (b * n_tiles + t, 0)),
        compiler_params=_cparams(2),
        name=name,
    )(q, k, v)


def _rwkv_prep_kernel(u_ref, prev_ref, next_ref, mu_ref, kk_ref, ka_ref, rk_ref, w0_ref, w2_ref,
                      a0_ref, a2_ref, g2_ref, bd_ref, tri_ref,
                      v_ref, bonus_ref, gate_ref, rt0_ref, at0_ref, kt0_ref, bt0_ref,
                      rt1_ref, at1_ref, kt1_ref, bt1_ref, pc_ref, *, n_tiles, n_ctx_tiles):
    t = pl.program_id(0) % n_tiles
    u = u_ref[...]
    rows = u.shape[0]
    row = lax.broadcasted_iota(jnp.int32, u.shape, 0)
    seq_start = jnp.logical_or(t == 0, t == n_ctx_tiles)
    seq_end = jnp.logical_or(t == n_ctx_tiles - 1, t == n_tiles - 1)
    halo_prev = prev_ref[7:8, :] * jnp.where(seq_start, 0.0, 1.0)
    halo_next = next_ref[0:1, :] * jnp.where(seq_end, 0.0, 1.0)
    prev = jnp.where(row == 0, halo_prev, pltpu.roll(u, 1, axis=0))
    nxt = jnp.where(row == rows - 1, halo_next, pltpu.roll(u, rows - 1, axis=0))
    us = u + mu_ref[0:1, :] * (prev - u) + mu_ref[1:2, :] * (nxt - u)

    c = RWKV_DIM
    r, k, v = us[:, 0:c], us[:, c:2 * c], us[:, 2 * c:3 * c]
    wd = jnp.tanh(us[:, 3 * c:3 * c + LANES]).astype(BF16)
    ad = us[:, 3 * c + LANES:3 * c + 2 * LANES].astype(BF16)
    gd = us[:, 3 * c + 2 * LANES:3 * c + 3 * LANES]
    bd = bd_ref[...]

    kk = k * kk_ref[...]
    kk = kk / jnp.maximum(jnp.sqrt(_group_sum(kk * kk, bd)), 1e-12)

    v_ref[...] = v
    gate_ref[...] = _dot(jax.nn.sigmoid(gd).astype(BF16), g2_ref[...])

    outs = ((rt0_ref, at0_ref, kt0_ref, bt0_ref), (rt1_ref, at1_ref, kt1_ref, bt1_ref))
    n_chunks = rows // CHUNK
    bonus = None
    for d in range(2):
        z = w0_ref[d:d + 1, :] + _dot(wd, w2_ref[d])
        nz = -z
        softplus = jnp.maximum(nz, 0.0) + jnp.log(1.0 + jnp.exp(-jnp.abs(nz)))
        lw = -jnp.exp(-softplus - 0.5)
        a = jax.nn.sigmoid(a0_ref[d:d + 1, :] + _dot(ad, a2_ref[d]))
        k_d = k * (1.0 + (a - 1.0) * ka_ref[...])
        b_d = _group_sum(r * k_d * rk_ref[...], bd) * v
        bonus = b_d if bonus is None else bonus + b_d

        tri = tri_ref[d]
        hi = lw.astype(BF16)
        rem = lw - hi.astype(F32)
        mid = rem.astype(BF16)
        lo = (rem - mid.astype(F32)).astype(BF16)
        cum = _dot(tri, hi) + (_dot(tri, mid) + _dot(tri, lo))
        e_neg = jnp.exp(-cum)
        rt_ref, at_ref, kt_ref, bt_ref = outs[d]
        rt_ref[...] = r * jnp.exp(cum)
        at_ref[...] = -kk * jnp.exp(cum - lw)
        kt_ref[...] = k_d * e_neg
        bt_ref[...] = kk * a * e_neg
        last = CHUNK - 1 if d == 0 else 0
        tot = jnp.concatenate([cum[ci * CHUNK + last:ci * CHUNK + last + 1, :] for ci in range(n_chunks)], axis=0)
        pc_ref[d * n_chunks:(d + 1) * n_chunks, :] = jnp.exp(tot)
    bonus_ref[...] = bonus


def _rwkv_prep_call(ru, mu, kk, ka, rk, w0, w2p, a0, a2p, g2, bd, tri, geom):
    n_rows, n_tiles, n_ctx_tiles, _ = geom
    grid = n_rows // ROW_TILE
    g8 = ROW_TILE // 8
    last8 = n_rows // 8 - 1
    kern = functools.partial(_rwkv_prep_kernel, n_tiles=n_tiles, n_ctx_tiles=n_ctx_tiles)
    c = RWKV_DIM
    n_pc = 2 * (ROW_TILE // CHUNK)
    return pl.pallas_call(
        kern,
        out_shape=[jax.ShapeDtypeStruct((n_rows, c), F32)] * 11
                  + [jax.ShapeDtypeStruct((grid * n_pc, c), F32)],
        grid=(grid,),
        in_specs=[
            _rows(RWKV_IN),
            pl.BlockSpec((8, RWKV_IN), lambda i: (jnp.maximum(i * g8 - 1, 0), 0)),
            pl.BlockSpec((8, RWKV_IN), lambda i: (jnp.minimum((i + 1) * g8, last8), 0)),
            _full(mu.shape), _full(kk.shape), _full(ka.shape), _full(rk.shape),
            _full(w0.shape), _full(w2p.shape), _full(a0.shape), _full(a2p.shape),
            _full(g2.shape), _full(bd.shape), _full(tri.shape),
        ],
        out_specs=[_rows(c)] * 11 + [pl.BlockSpec((n_pc, c), lambda i: (i, 0))],
        compiler_params=_cparams(1),
        name="rwkv_prep",
    )(ru, ru, ru, mu, kk, ka, rk, w0, w2p, a0, a2p, g2, bd, tri)


def _rwkv_scan_kernel(*refs):
    ins, (y0_ref, y1_ref, s_ref) = refs[:12], refs[12:]
    y_refs = (y0_ref, y1_ref)

    @pl.when(pl.program_id(1) == 0)
    def _():
        s_ref[...] = jnp.zeros_like(s_ref)

    n = RWKV_HEAD
    ri = lax.broadcasted_iota(jnp.int32, (2 * CHUNK, 2 * CHUNK), 0)
    ci = lax.broadcasted_iota(jnp.int32, (2 * CHUNK, 2 * CHUNK), 1)
    tq, ts = ri % CHUNK, ci % CHUNK
    eye = (lax.broadcasted_iota(jnp.int32, (CHUNK, CHUNK), 0)
           == lax.broadcasted_iota(jnp.int32, (CHUNK, CHUNK), 1)).astype(F32)

    chains = [(d, hd) for d in range(2) for hd in range(RWKV_HEADS)]
    masks, lhs, rhs, vv, pc, s_all = [], [], [], [], [], []
    for d in range(2):
        rt_ref, at_ref, kt_ref, bt_ref, v_ref, pc_ref = ins[6 * d:6 * d + 6]
        earlier = (ts < tq) if d == 0 else (ts > tq)
        masks.append(jnp.where(jnp.logical_or(earlier, jnp.logical_and(ri >= CHUNK, ts == tq)), 1.0, 0.0))
        lhs.append(jnp.concatenate([at_ref[...], rt_ref[...]], axis=0))
        rhs.append(jnp.concatenate([bt_ref[...], kt_ref[...]], axis=0))
        vv.append(v_ref[...])
        pc.append(pc_ref[...])
        s_all.append(s_ref[d])

    def head(arr, hd):
        return arr[:, hd * n:(hd + 1) * n]

    lhs_h = [_split(head(lhs[d], hd)) for d, hd in chains]
    rhs_h = [_split(head(rhs[d], hd)) for d, hd in chains]
    gram = [_dot3(l_, r_, _dot_nt) * masks[d] for (d, _), l_, r_ in zip(chains, lhs_h, rhs_h)]
    x = [g[0:CHUNK, 0:CHUNK] for g in gram]
    tm = [eye + xi for xi in x]
    x_sp = [_split(xi) for xi in x]
    for _ in range(CHUNK.bit_length() - 2):
        x = [_dot3(xs_, xs_) for xs_ in x_sp]
        x_sp = [_split(xi) for xi in x]
        tm = [ti + _dot3(_split(ti), xs_) for ti, xs_ in zip(tm, x_sp)]
    s_h = [head(s_all[d], hd) for d, hd in chains]
    s_sp = [_split(si) for si in s_h]
    v_h = [head(vv[d], hd) for d, hd in chains]
    v_sp = [_split(vi) for vi in v_h]
    a_ak = [_split(g[0:CHUNK, CHUNK:]) for g in gram]
    xin = [_dot3((l_[0][0:CHUNK], l_[1][0:CHUNK]), ss, _dot_nt) + _dot3(ak, vs)
           for l_, ss, ak, vs in zip(lhs_h, s_sp, a_ak, v_sp)]
    u = [_dot3(_split(ti), _split(xi)) for ti, xi in zip(tm, xin)]
    uv_sp = [_split(jnp.concatenate([ui, vi], axis=0)) for ui, vi in zip(u, v_h)]
    a_r = [_split(g[CHUNK:, :]) for g in gram]
    y = [_dot3((l_[0][CHUNK:], l_[1][CHUNK:]), ss, _dot_nt) + _dot3(ar, uvs)
         for l_, ss, ar, uvs in zip(lhs_h, s_sp, a_r, uv_sp)]
    s_new = [(si + _dot3(uvs, r_, _dot_tn)) * head(pc[d], hd)
             for (d, hd), si, uvs, r_ in zip(chains, s_h, uv_sp, rhs_h)]
    for d in range(2):
        y_refs[d][...] = jnp.concatenate([y[d * RWKV_HEADS + hd] for hd in range(RWKV_HEADS)], axis=1)
        s_ref[d] = jnp.concatenate([s_new[d * RWKV_HEADS + hd] for hd in range(RWKV_HEADS)], axis=1)


def _rwkv_scan_call(prep, batch, s_tot, ctx_len):
    v, _, _, rt0, at0, kt0, bt0, rt1, at1, kt1, bt1, pc = prep
    n_ch = s_tot // CHUNK
    n_ctx_ch = ctx_len // CHUNK
    per_tile = ROW_TILE // CHUNK
    c = RWKV_DIM

    def chunk_of(d, i):
        if d == 0:
            return i
        return jnp.where(i < n_ctx_ch, n_ctx_ch - 1 - i, n_ctx_ch + n_ch - 1 - i)

    def row_spec(d):
        return pl.BlockSpec((CHUNK, c), lambda b, i: (b * n_ch + chunk_of(d, i), 0))

    def pc_spec(d):
        def index(b, i):
            ch = b * n_ch + chunk_of(d, i)
            return ((ch // per_tile) * 2 * per_tile + d * per_tile + ch % per_tile, 0, 0)
        return pl.BlockSpec((None, 1, c), index)

    pc3 = pc.reshape(pc.shape[0], 1, c)
    ins, specs = [], []
    for d, arrs in enumerate(((rt0, at0, kt0, bt0), (rt1, at1, kt1, bt1))):
        ins += list(arrs) + [v, pc3]
        specs += [row_spec(d)] * 5 + [pc_spec(d)]
    return pl.pallas_call(
        _rwkv_scan_kernel,
        out_shape=[jax.ShapeDtypeStruct((batch * s_tot, c), F32)] * 2,
        grid=(batch, n_ch),
        in_specs=specs,
        out_specs=[row_spec(0), row_spec(1)],
        scratch_shapes=[pltpu.VMEM((2, RWKV_HEAD, c), F32)],
        compiler_params=_cparams(2),
        name="rwkv_scan",
    )(*ins)


def _merge_kernel(x_ref, g1_ref, sh_ref, sc_ref, gt_ref, a_ref, y0_ref, y1_ref, bonus_ref, rg_ref, m_ref,
                  lnw_ref, lnb_ref, bd_ref, wg_ref, wb_ref, wo_ref, o_ref):
    x = x_ref[...]
    h = (_rms(x, g1_ref[...]) * (1.0 + sc_ref[...]) + sh_ref[...]).astype(BF16)

    bd = bd_ref[...]
    y = y0_ref[...] + y1_ref[...]
    mu = _group_sum(y, bd) * (1.0 / RWKV_HEAD)
    yc = y - mu
    var = _group_sum(yc * yc, bd) * (1.0 / RWKV_HEAD)
    yn = yc * lax.rsqrt(var + LNX_EPS) * lnw_ref[...] + lnb_ref[...]
    r_out = ((yn + bonus_ref[...]) * rg_ref[...]).astype(BF16)

    d = D_MODEL
    mixed = None
    for i, yb in enumerate((a_ref[...], r_out, m_ref[...])):
        gate = jax.nn.sigmoid(_dot(h, wg_ref[:, i * d:(i + 1) * d]))
        term = gate * _dot(yb, wb_ref[i])
        mixed = term if mixed is None else mixed + term
    o_ref[...] = x + gt_ref[...] * _dot(mixed.astype(BF16), wo_ref[...])


def _merge_call(xs, mod, g1, a, y0, y1, bonus, rg, m, lnw, lnb, bd, wg, wb, wo, geom):
    n_rows, n_tiles, n_ctx_tiles, ctx_row = geom
    ms = lambda which: _row_mod_spec(which, n_tiles, n_ctx_tiles, ctx_row)
    c = RWKV_DIM
    return pl.pallas_call(
        _merge_kernel,
        out_shape=jax.ShapeDtypeStruct((n_rows, D_MODEL), F32),
        grid=(n_rows // ROW_TILE,),
        in_specs=[
            _rows(D_MODEL), _full((1, D_MODEL)), ms(0), ms(1), ms(2),
            _rows(c), _rows(c), _rows(c), _rows(c), _rows(c), _rows(c),
            _full(lnw.shape), _full(lnb.shape), _full(bd.shape),
            _full(wg.shape), _full(wb.shape), _full(wo.shape),
        ],
        out_specs=_rows(D_MODEL),
        compiler_params=_cparams(1),
        name="merge",
    )(xs, g1, mod, mod, mod, a, y0, y1, bonus, rg, m, lnw, lnb, bd, wg, wb, wo)


def _mlp_kernel(x_ref, g2_ref, sh_ref, sc_ref, gt_ref, w1_ref, w2_ref, gf_ref, o_ref, *, final_norm):
    x = x_ref[...]
    h = (_rms(x, g2_ref[...]) * (1.0 + sc_ref[...]) + sh_ref[...]).astype(BF16)
    acc = None
    blk = D_MODEL
    for j in range(D_FF // blk):
        a = jnp.maximum(_dot(h, w1_ref[:, j * blk:(j + 1) * blk]), 0.0)
        part = _dot((a * a).astype(BF16), w2_ref[j * blk:(j + 1) * blk, :])
        acc = part if acc is None else acc + part
    out = x + gt_ref[...] * acc
    if final_norm:
        out = _rms(out, gf_ref[...])
    o_ref[...] = out


def _mlp_call(xs, mod, g2, w1, w2, gf, geom, final_norm):
    n_rows, n_tiles, n_ctx_tiles, ctx_row = geom
    ms = lambda which: _row_mod_spec(which, n_tiles, n_ctx_tiles, ctx_row)
    return pl.pallas_call(
        functools.partial(_mlp_kernel, final_norm=final_norm),
        out_shape=jax.ShapeDtypeStruct((n_rows, D_MODEL), F32),
        grid=(n_rows // ROW_TILE,),
        in_specs=[
            _rows(D_MODEL), _full((1, D_MODEL)), ms(3), ms(4), ms(5),
            _full(w1.shape), _full(w2.shape), _full((1, D_MODEL)),
        ],
        out_specs=_rows(D_MODEL),
        compiler_params=_cparams(1),
        name="mlp",
    )(xs, g2, mod, mod, mod, w1, w2, gf)


def _rope_tables(ctx_len, seq):
    pos = jnp.arange(seq, dtype=jnp.int32)
    rows, cols = pos // GRID_W, pos % GRID_W

    def cs(p, n):
        inv = ROPE_THETA ** (-jnp.arange(n, dtype=F32) / n)
        ang = p.astype(F32)[:, None] * inv[None, :]
        return jnp.cos(ang), jnp.sin(ang)

    def axial(n):
        cr, sr = cs(rows, n)
        cc, sc = cs(cols, n)
        return jnp.concatenate([cr, cr, cc, cc], -1), jnp.concatenate([-sr, sr, -sc, sc], -1)

    def with_ctx(cos, sin):
        w = cos.shape[1]
        return (jnp.concatenate([jnp.ones((ctx_len, w), F32), cos], 0),
                jnp.concatenate([jnp.zeros((ctx_len, w), F32), sin], 0))

    gc, gs = axial(GQA_HEAD_DIM // 4)
    gc, gs = jnp.tile(gc, (1, 2)), jnp.tile(gs, (1, 2))
    mc, ms = axial(QK_ROPE // 4)
    pad = MLA_SLOT - QK_NOPE - QK_ROPE
    mc = jnp.concatenate([jnp.ones((seq, QK_NOPE), F32), mc, jnp.ones((seq, pad), F32)], -1)
    ms = jnp.concatenate([jnp.zeros((seq, QK_NOPE), F32), ms, jnp.zeros((seq, pad), F32)], -1)
    return with_ctx(gc, gs) + with_ctx(mc, ms)


def _block_ones(n, blk):
    i = jnp.arange(n)
    return (i[:, None] // blk == i[None, :] // blk)


def _chunk_tri():
    i = jnp.arange(ROW_TILE)
    same = _block_ones(ROW_TILE, CHUNK)
    fwd = jnp.logical_and(same, i[None, :] <= i[:, None])
    bwd = jnp.logical_and(same, i[None, :] >= i[:, None])
    return jnp.stack([fwd, bwd]).astype(BF16)


def _pack_layer(l, w_in, gqa_q_gain, gqa_k_gain, mla_q_up, mla_kv_up, rwkv_w2, rwkv_a2):
    w = w_in[l]
    o_q = 0
    o_rw = GQA_HEADS * GQA_HEAD_DIM + 2 * GQA_KV_HEADS * GQA_HEAD_DIM
    o_qd = o_rw + RWKV_IN
    o_kr = o_qd + Q_LORA + KV_LORA
    o_gate = o_kr + QK_ROPE
    zeros = lambda n: jnp.zeros((D_MODEL, n), F32)
    o_v = o_rw - GQA_KV_HEADS * GQA_HEAD_DIM
    w_v = jnp.pad(w[:, o_v:o_rw].reshape(D_MODEL, GQA_KV_HEADS, GQA_HEAD_DIM),
                  ((0, 0), (0, 0), (0, V_SLOT - GQA_HEAD_DIM))).reshape(D_MODEL, GQA_KV_HEADS * V_SLOT)
    w_main = jnp.concatenate(
        [w[:, o_q:o_v], w_v, w[:, o_rw:o_kr], zeros(QK_NOPE), w[:, o_kr:o_gate],
         zeros(MLA_SLOT - QK_NOPE - QK_ROPE)], axis=1)
    w_gate = w[:, o_gate:]
    gains = jnp.concatenate([jnp.tile(gqa_q_gain[l], GQA_HEADS), jnp.tile(gqa_k_gain[l], GQA_KV_HEADS)])[None, :]

    qup = mla_q_up[l].reshape(Q_LORA, MLA_HEADS, QK_NOPE + QK_ROPE)
    qup = jnp.pad(qup, ((0, 0), (0, 0), (0, MLA_SLOT - QK_NOPE - QK_ROPE))).reshape(Q_LORA, MLA_HEADS * MLA_SLOT)
    kvu = mla_kv_up[l].reshape(KV_LORA, MLA_HEADS, QK_NOPE + V_HEAD)
    kvk = jnp.pad(kvu[:, :, :QK_NOPE], ((0, 0), (0, 0), (0, MLA_SLOT - QK_NOPE))).reshape(KV_LORA, MLA_HEADS * MLA_SLOT)
    kvv = jnp.pad(kvu[:, :, QK_NOPE:], ((0, 0), (0, 0), (0, V_SLOT - V_HEAD))).reshape(KV_LORA, MLA_HEADS * V_SLOT)

    def pad_dir(w2):
        lora = w2.shape[1]
        z = jnp.zeros_like(w2[0])
        return jnp.stack([jnp.concatenate([w2[0], z], 0), jnp.concatenate([z, w2[1]], 0)]).astype(BF16)

    return dict(w_main=w_main.astype(BF16), w_gate=w_gate.astype(BF16), gains=gains,
                qup=qup.astype(BF16), kvk=kvk.astype(BF16), kvv=kvv.astype(BF16),
                w2p=pad_dir(rwkv_w2[l]), a2p=pad_dir(rwkv_a2[l]))


def kernel(x, c, ctx, c_ctx, w_mod, b_mod, g_norm1, g_norm2, w_in, gqa_q_gain, gqa_k_gain, rwkv_shift_mu, rwkv_w0, rwkv_w2, rwkv_a0, rwkv_a2, rwkv_g2, rwkv_k_k, rwkv_k_a, rwkv_r_k, rwkv_ln_w, rwkv_ln_b, mla_q_norm, mla_q_up, mla_kv_norm, mla_kv_up, w_branch, w_out, w_ff1, w_ff2, g_final):
    batch, seq, d = x.shape
    ctx_len = ctx.shape[1]
    depth = w_mod.shape[0]
    s_tot = ctx_len + seq
    assert d == D_MODEL and batch < 8
    assert ctx_len % ROW_TILE == 0 and seq % ROW_TILE == 0 and seq % GRID_W == 0
    n_rows = batch * s_tot
    geom = (n_rows, s_tot // ROW_TILE, ctx_len // ROW_TILE, batch)

    c_all = jnp.concatenate([c, c_ctx[None, :], jnp.zeros((8 - batch - 1, d), F32)], axis=0)
    mod_all = _mod_call(c_all, w_mod, b_mod).reshape(depth, 8, 6, 1, d)

    tabs = _rope_tables(ctx_len, seq)
    bd = _block_ones(LANES, RWKV_HEAD).astype(BF16)
    tri = _chunk_tri()
    row1 = lambda v: v.reshape(1, -1)

    xs = jnp.concatenate([ctx, x], axis=1).reshape(n_rows, d)
    for l in range(depth):
        pk = _pack_layer(l, w_in, gqa_q_gain, gqa_k_gain, mla_q_up, mla_kv_up, rwkv_w2, rwkv_a2)
        mod = mod_all[l]
        g1 = row1(g_norm1[l])
        gq, gk, gv, ru, mq, mk, mv = _inproj_call(
            xs, mod, g1, pk["w_main"], pk["gains"], bd, tabs,
            row1(mla_q_norm[l]), pk["qup"], row1(mla_kv_norm[l]), pk["kvk"], pk["kvv"], geom)
        a_out = _attn_call(gq, gk, gv, heads=GQA_HEADS, group=GQA_HEADS // GQA_KV_HEADS, dq=GQA_HEAD_DIM,
                           dv=GQA_HEAD_DIM, scale=1.0, batch=batch, s_tot=s_tot, ctx_len=ctx_len, name="gqa_attn")
        m_out = _attn_call(mq, mk, mv, heads=MLA_HEADS, group=1, dq=MLA_SLOT, dv=V_HEAD, scale=MLA_SCALE,
                           batch=batch, s_tot=s_tot, ctx_len=ctx_len, name="mla_attn")
        prep = _rwkv_prep_call(ru, rwkv_shift_mu[l], row1(rwkv_k_k[l]), row1(rwkv_k_a[l]), row1(rwkv_r_k[l]),
                               rwkv_w0[l], pk["w2p"], rwkv_a0[l], pk["a2p"], rwkv_g2[l].astype(BF16), bd, tri, geom)
        y0, y1 = _rwkv_scan_call(prep, batch, s_tot, ctx_len)
        xs = _merge_call(xs, mod, g1, a_out, y0, y1, prep[1], prep[2], m_out,
                         row1(rwkv_ln_w[l]), row1(rwkv_ln_b[l]), bd, pk["w_gate"],
                         w_branch[l].astype(BF16), w_out[l].astype(BF16), geom)
        xs = _mlp_call(xs, mod, row1(g_norm2[l]), w_ff1[l].astype(BF16), w_ff2[l].astype(BF16),
                       row1(g_final), geom, final_norm=(l == depth - 1))
    return xs.reshape(batch, s_tot, d)[:, ctx_len:, :]
```

```python
import functools

import jax
import jax.numpy as jnp
from jax import lax
from jax.experimental import pallas as pl
from jax.experimental.pallas import tpu as pltpu

F32 = jnp.float32
BF16 = jnp.bfloat16

D_MODEL = 1024
GRID_W = 64
ROPE_THETA = 10000.0
NORM_EPS = 1e-6

GQA_HEADS = 8
GQA_KV_HEADS = 2
GQA_HEAD_DIM = 64
GQA_SCALE = GQA_HEAD_DIM ** -0.5

RWKV_HEADS = 8
RWKV_HEAD = 64
RWKV_DIM = RWKV_HEADS * RWKV_HEAD
DECAY_LORA = 64
AAA_LORA = 64
GATE_LORA = 128
LNX_EPS = 64e-5
RWKV_IN = 3 * RWKV_DIM + 2 * DECAY_LORA + 2 * AAA_LORA + GATE_LORA

MLA_HEADS = 8
Q_LORA = 384
KV_LORA = 256
QK_NOPE = 64
QK_ROPE = 32
V_HEAD = 64
MLA_SCALE = (QK_NOPE + QK_ROPE) ** -0.5
MLA_SLOT = 128
V_SLOT = 128
ATTN_KEY_BLOCK = 512
LOG2_E = 1.4426950408889634

N_BRANCH = 3
BRANCH_W = 512
D_FF = 4 * D_MODEL

LANES = 128
ROW_TILE = 256
CHUNK = 64
NEUMANN_SINGLE_PASS_LEVELS = 1
VMEM_LIMIT = 56 * 1024 * 1024

_C_GQ = 0
_C_GK = _C_GQ + GQA_HEADS * GQA_HEAD_DIM
_C_GV = _C_GK + GQA_KV_HEADS * GQA_HEAD_DIM
_C_RW = _C_GV + GQA_KV_HEADS * V_SLOT
_C_QD = _C_RW + RWKV_IN
_C_KVD = _C_QD + Q_LORA
_C_KR = _C_KVD + KV_LORA
_C_END = _C_KR + LANES


def _cparams(n_axes):
    return pltpu.CompilerParams(dimension_semantics=("arbitrary",) * n_axes,
                                vmem_limit_bytes=VMEM_LIMIT)


def _dot(a, b):
    return jnp.dot(a, b, preferred_element_type=F32)


def _dot_nt(a, b):
    return lax.dot_general(a, b, (((1,), (1,)), ((), ())), preferred_element_type=F32)


def _dot_tn(a, b):
    return lax.dot_general(a, b, (((0,), (0,)), ((), ())), preferred_element_type=F32)


def _split(x):
    hi = x.astype(BF16)
    lo = (x - hi.astype(F32)).astype(BF16)
    return hi, lo


def _dot3(a, b, dot=_dot):
    m = a[0].shape[0]
    both = dot(jnp.concatenate([a[0], a[1]], axis=0), b[0])
    return both[0:m] + both[m:] + dot(a[0], b[1])


def _dot3_tn(a, b):
    m = a[0].shape[1]
    both = _dot_tn(jnp.concatenate([a[0], a[1]], axis=1), b[0])
    return both[0:m] + both[m:] + _dot_tn(a[0], b[1])


def _group_sum(x, bd):
    outs = []
    for g in range(x.shape[1] // LANES):
        hi, lo = _split(x[:, g * LANES:(g + 1) * LANES])
        outs.append(_dot(hi, bd) + _dot(lo, bd))
    return outs[0] if len(outs) == 1 else jnp.concatenate(outs, axis=1)


def _rope(x, cos, sin, half):
    lane = lax.broadcasted_iota(jnp.int32, x.shape, 1)
    up = pltpu.roll(x, LANES - half, axis=1)
    dn = pltpu.roll(x, half, axis=1)
    sw = jnp.where((lane % (2 * half)) < half, up, dn)
    return x * cos + sw * sin


def _with_ones_lane(v):
    lane = lax.broadcasted_iota(jnp.int32, v.shape, 1)
    return jnp.where(lane % V_SLOT == V_HEAD, 1.0, v)


def _rms(x, gain):
    ms = jnp.mean(x * x, axis=-1, keepdims=True)
    return x * lax.rsqrt(ms + NORM_EPS) * gain


def _mod_kernel(c_ref, w_ref, b_ref, o_ref):
    c = c_ref[...]
    s = (c * jax.nn.sigmoid(c)).astype(BF16)
    o_ref[...] = _dot(s, w_ref[...].astype(BF16)) + b_ref[...]


def _mod_call(c_all, w_mod, b_mod):
    n_layers, d, n = w_mod.shape
    bn = 1536
    return pl.pallas_call(
        _mod_kernel,
        out_shape=jax.ShapeDtypeStruct((n_layers, 8, n), F32),
        grid=(n_layers, n // bn),
        in_specs=[
            pl.BlockSpec((8, d), lambda l, j: (0, 0)),
            pl.BlockSpec((None, d, bn), lambda l, j: (l, 0, j)),
            pl.BlockSpec((None, 1, bn), lambda l, j: (l, 0, j)),
        ],
        out_specs=pl.BlockSpec((None, 8, bn), lambda l, j: (l, 0, j)),
        compiler_params=_cparams(2),
        name="adaln_mod",
    )(c_all, w_mod, b_mod.reshape(n_layers, 1, n))


def _inproj_kernel(x_ref, g1_ref, sh_ref, sc_ref, w_ref, gain_ref, bd_ref,
                   gc_ref, gs_ref, mc_ref, ms_ref,
                   qn_ref, qup_ref, kvn_ref, kvk_ref, kvv_ref,
                   gq_ref, gk_ref, gv_ref, ru_ref, mq_ref, mk_ref, mv_ref):
    x = x_ref[...]
    h = _rms(x, g1_ref[...]) * (1.0 + sc_ref[...]) + sh_ref[...]
    p = _dot(h.astype(BF16), w_ref[...])

    bd = bd_ref[...]
    gc, gs = gc_ref[...], gs_ref[...]
    n_q_slabs = GQA_HEADS * GQA_HEAD_DIM // LANES
    n_k_slabs = GQA_KV_HEADS * GQA_HEAD_DIM // LANES
    for g in range(n_q_slabs + n_k_slabs):
        slab = p[:, g * LANES:(g + 1) * LANES]
        ms = _group_sum(slab * slab, bd) * (1.0 / GQA_HEAD_DIM)
        y = slab * lax.rsqrt(ms + NORM_EPS) * gain_ref[:, g * LANES:(g + 1) * LANES]
        y = _rope(y, gc, gs, GQA_HEAD_DIM // 4)
        if g < n_q_slabs:
            gq_ref[:, g * LANES:(g + 1) * LANES] = (y * (GQA_SCALE * LOG2_E)).astype(BF16)
        else:
            gk_ref[:, (g - n_q_slabs) * LANES:(g - n_q_slabs + 1) * LANES] = y.astype(BF16)
    gv_ref[...] = _with_ones_lane(p[:, _C_GV:_C_RW]).astype(BF16)

    ru_ref[...] = p[:, _C_RW:_C_QD]

    mc, msn = mc_ref[...], ms_ref[...]
    qd = _rms(p[:, _C_QD:_C_KVD], qn_ref[...])
    q = _dot(qd.astype(BF16), qup_ref[...])
    kvd = _rms(p[:, _C_KVD:_C_KR], kvn_ref[...]).astype(BF16)
    kn = _dot(kvd, kvk_ref[...])
    kr = _rope(p[:, _C_KR:_C_END], mc, msn, QK_ROPE // 4)
    for hd in range(MLA_HEADS):
        sl = slice(hd * MLA_SLOT, (hd + 1) * MLA_SLOT)
        mq_ref[:, sl] = (_rope(q[:, sl], mc, msn, QK_ROPE // 4) * (MLA_SCALE * LOG2_E)).astype(BF16)
        mk_ref[:, sl] = (kn[:, sl] + kr).astype(BF16)
    mv_ref[...] = _with_ones_lane(_dot(kvd, kvv_ref[...])).astype(BF16)


def _row_mod_spec(which, n_tiles, n_ctx_tiles, ctx_row):
    def index(i):
        b, t = i // n_tiles, i % n_tiles
        return (jnp.where(t < n_ctx_tiles, ctx_row, b), which, 0, 0)
    return pl.BlockSpec((None, None, 1, D_MODEL), index)


def _full(shape):
    return pl.BlockSpec(shape, lambda i: (0,) * len(shape))


def _rows(width):
    return pl.BlockSpec((ROW_TILE, width), lambda i: (i, 0))


def _inproj_call(xs, mod, g1, w_main, gains, bd, tabs, qn, qup, kvn, kvk, kvv, geom):
    n_rows, n_tiles, n_ctx_tiles, ctx_row = geom
    tab_spec = pl.BlockSpec((ROW_TILE, LANES), lambda i: (i % n_tiles, 0))
    widths = (GQA_HEADS * GQA_HEAD_DIM, GQA_KV_HEADS * GQA_HEAD_DIM, GQA_KV_HEADS * V_SLOT,
              RWKV_IN, MLA_HEADS * MLA_SLOT, MLA_HEADS * MLA_SLOT, MLA_HEADS * V_SLOT)
    dtypes = (BF16, BF16, BF16, F32, BF16, BF16, BF16)
    return pl.pallas_call(
        _inproj_kernel,
        out_shape=[jax.ShapeDtypeStruct((n_rows, w), dt) for w, dt in zip(widths, dtypes)],
        grid=(n_rows // ROW_TILE,),
        in_specs=[
            _rows(D_MODEL), _full((1, D_MODEL)),
            _row_mod_spec(0, n_tiles, n_ctx_tiles, ctx_row),
            _row_mod_spec(1, n_tiles, n_ctx_tiles, ctx_row),
            _full(w_main.shape), _full(gains.shape), _full(bd.shape),
            tab_spec, tab_spec, tab_spec, tab_spec,
            _full(qn.shape), _full(qup.shape), _full(kvn.shape), _full(kvk.shape), _full(kvv.shape),
        ],
        out_specs=[_rows(w) for w in widths],
        compiler_params=_cparams(1),
        name="in_proj",
    )(xs, g1, mod, mod, w_main, gains, bd, *tabs, qn, qup, kvn, kvk, kvv)


def _attn_kernel(q_ref, k_ref, v_ref, o_ref, s_ref, e_ref, *, heads, group, dq, dv, n_ctx_tiles, ctx_len, s_tot):
    t = pl.program_id(1)

    def run(nk):
        blocks = [(lo, min(lo + ATTN_KEY_BLOCK, nk)) for lo in range(0, nk, ATTN_KEY_BLOCK)]
        for hq in range(heads):
            hk = hq // group
            q = q_ref[:, hq * dq:(hq + 1) * dq]
            mrun = None
            for lo, hi in blocks:
                sj = _dot_nt(q, k_ref[lo:hi, hk * dq:(hk + 1) * dq])
                s_ref[:, lo:hi] = sj
                for c0 in range(0, hi - lo, LANES):
                    piece = sj[:, c0:c0 + LANES]
                    mrun = piece if mrun is None else jnp.maximum(mrun, piece)
            m = jnp.max(mrun, axis=-1, keepdims=True)
            for lo, hi in blocks:
                e_ref[:, lo:hi] = jnp.exp2(s_ref[:, lo:hi] - m).astype(BF16)
            v = v_ref[0:nk, hk * V_SLOT:(hk + 1) * V_SLOT]
            ov = _dot(e_ref[:, 0:nk], v)
            o_ref[:, hq * dv:(hq + 1) * dv] = (ov[:, 0:dv] / ov[:, dv:dv + 1]).astype(o_ref.dtype)

    @pl.when(t < n_ctx_tiles)
    def _():
        run(ctx_len)

    @pl.when(t >= n_ctx_tiles)
    def _():
        run(s_tot)


def _attn_call(q, k, v, *, heads, group, dq, dv, batch, s_tot, ctx_len, name):
    n_tiles = s_tot // ROW_TILE
    assert ctx_len % LANES == 0 and s_tot % LANES == 0
    kern = functools.partial(_attn_kernel, heads=heads, group=group, dq=dq, dv=dv,
                             n_ctx_tiles=ctx_len // ROW_TILE, ctx_len=ctx_len, s_tot=s_tot)
    return pl.pallas_call(
        kern,
        out_shape=jax.ShapeDtypeStruct((batch * s_tot, heads * dv), BF16),
        grid=(batch, n_tiles),
        in_specs=[
            pl.BlockSpec((ROW_TILE, q.shape[1]), lambda b, t: (b * n_tiles + t, 0)),
            pl.BlockSpec((s_tot, k.shape[1]), lambda b, t: (b, 0)),
            pl.BlockSpec((s_tot, v.shape[1]), lambda b, t: (b, 0)),
        ],
        out_specs=pl.BlockSpec((ROW_TILE, heads * dv), lambda b, t: (b * n_tiles + t, 0)),
        scratch_shapes=[pltpu.VMEM((ROW_TILE, s_tot), F32), pltpu.VMEM((ROW_TILE, s_tot), BF16)],
        compiler_params=_cparams(2),
        name=name,
    )(q, k, v)


def _rwkv_prep_kernel(u_ref, prev_ref, next_ref, mu_ref, kk_ref, ka_ref, rk_ref, w0_ref, w2_ref,
                      a0_ref, a2_ref, g2_ref, bd_ref, tri_ref,
                      v_ref, bonus_ref, gate_ref, rt0_ref, at0_ref, kt0_ref, bt0_ref,
                      rt1_ref, at1_ref, kt1_ref, bt1_ref, pc_ref, *, n_tiles, n_ctx_tiles):
    t = pl.program_id(0) % n_tiles
    u = u_ref[...]
    rows = u.shape[0]
    row = lax.broadcasted_iota(jnp.int32, u.shape, 0)
    seq_start = jnp.logical_or(t == 0, t == n_ctx_tiles)
    seq_end = jnp.logical_or(t == n_ctx_tiles - 1, t == n_tiles - 1)
    halo_prev = prev_ref[7:8, :] * jnp.where(seq_start, 0.0, 1.0)
    halo_next = next_ref[0:1, :] * jnp.where(seq_end, 0.0, 1.0)
    prev = jnp.where(row == 0, halo_prev, pltpu.roll(u, 1, axis=0))
    nxt = jnp.where(row == rows - 1, halo_next, pltpu.roll(u, rows - 1, axis=0))
    us = u + mu_ref[0:1, :] * (prev - u) + mu_ref[1:2, :] * (nxt - u)

    c = RWKV_DIM
    r, k, v = us[:, 0:c], us[:, c:2 * c], us[:, 2 * c:3 * c]
    wd = jnp.tanh(us[:, 3 * c:3 * c + LANES]).astype(BF16)
    ad = us[:, 3 * c + LANES:3 * c + 2 * LANES].astype(BF16)
    gd = us[:, 3 * c + 2 * LANES:3 * c + 3 * LANES]
    bd = bd_ref[...]

    kk = k * kk_ref[...]
    kk = kk / jnp.maximum(jnp.sqrt(_group_sum(kk * kk, bd)), 1e-12)

    v_ref[...] = v
    gate_ref[...] = _dot(jax.nn.sigmoid(gd).astype(BF16), g2_ref[...])

    outs = ((rt0_ref, at0_ref, kt0_ref, bt0_ref), (rt1_ref, at1_ref, kt1_ref, bt1_ref))
    n_chunks = rows // CHUNK
    bonus = None
    for d in range(2):
        z = w0_ref[d:d + 1, :] + _dot(wd, w2_ref[d])
        nz = -z
        softplus = jnp.maximum(nz, 0.0) + jnp.log(1.0 + jnp.exp(-jnp.abs(nz)))
        lw = -jnp.exp(-softplus - 0.5)
        a = jax.nn.sigmoid(a0_ref[d:d + 1, :] + _dot(ad, a2_ref[d]))
        k_d = k * (1.0 + (a - 1.0) * ka_ref[...])
        b_d = _group_sum(r * k_d * rk_ref[...], bd) * v
        bonus = b_d if bonus is None else bonus + b_d

        tri = tri_ref[d]
        hi = lw.astype(BF16)
        rem = lw - hi.astype(F32)
        mid = rem.astype(BF16)
        lo = (rem - mid.astype(F32)).astype(BF16)
        cum = _dot(tri, hi) + (_dot(tri, mid) + _dot(tri, lo))
        e_neg = jnp.exp(-cum)
        rt_ref, at_ref, kt_ref, bt_ref = outs[d]
        rt_ref[...] = r * jnp.exp(cum)
        at_ref[...] = -kk * jnp.exp(cum - lw)
        kt_ref[...] = k_d * e_neg
        bt_ref[...] = kk * a * e_neg
        last = CHUNK - 1 if d == 0 else 0
        tot = jnp.concatenate([cum[ci * CHUNK + last:ci * CHUNK + last + 1, :] for ci in range(n_chunks)], axis=0)
        pc_ref[d * n_chunks:(d + 1) * n_chunks, :] = jnp.exp(tot)
    bonus_ref[...] = bonus


def _rwkv_prep_call(ru, mu, kk, ka, rk, w0, w2p, a0, a2p, g2, bd, tri, geom):
    n_rows, n_tiles, n_ctx_tiles, _ = geom
    grid = n_rows // ROW_TILE
    g8 = ROW_TILE // 8
    last8 = n_rows // 8 - 1
    kern = functools.partial(_rwkv_prep_kernel, n_tiles=n_tiles, n_ctx_tiles=n_ctx_tiles)
    c = RWKV_DIM
    n_pc = 2 * (ROW_TILE // CHUNK)
    return pl.pallas_call(
        kern,
        out_shape=[jax.ShapeDtypeStruct((n_rows, c), F32)] * 11
                  + [jax.ShapeDtypeStruct((grid * n_pc, c), F32)],
        grid=(grid,),
        in_specs=[
            _rows(RWKV_IN),
            pl.BlockSpec((8, RWKV_IN), lambda i: (jnp.maximum(i * g8 - 1, 0), 0)),
            pl.BlockSpec((8, RWKV_IN), lambda i: (jnp.minimum((i + 1) * g8, last8), 0)),
            _full(mu.shape), _full(kk.shape), _full(ka.shape), _full(rk.shape),
            _full(w0.shape), _full(w2p.shape), _full(a0.shape), _full(a2p.shape),
            _full(g2.shape), _full(bd.shape), _full(tri.shape),
        ],
        out_specs=[_rows(c)] * 11 + [pl.BlockSpec((n_pc, c), lambda i: (i, 0))],
        compiler_params=_cparams(1),
        name="rwkv_prep",
    )(ru, ru, ru, mu, kk, ka, rk, w0, w2p, a0, a2p, g2, bd, tri)


def _rwkv_scan_kernel(*refs):
    ins, (y0_ref, y1_ref, s_ref) = refs[:12], refs[12:]
    y_refs = (y0_ref, y1_ref)

    @pl.when(pl.program_id(1) == 0)
    def _():
        s_ref[...] = jnp.zeros_like(s_ref)

    n = RWKV_HEAD
    ri = lax.broadcasted_iota(jnp.int32, (2 * CHUNK, 2 * CHUNK), 0)
    ci = lax.broadcasted_iota(jnp.int32, (2 * CHUNK, 2 * CHUNK), 1)
    tq, ts = ri % CHUNK, ci % CHUNK

    chains = [(d, hd) for d in range(2) for hd in range(RWKV_HEADS)]
    masks, lhs, rhs, vv, pc, s_all = [], [], [], [], [], []
    for d in range(2):
        rt_ref, at_ref, kt_ref, bt_ref, v_ref, pc_ref = ins[6 * d:6 * d + 6]
        earlier = (ts < tq) if d == 0 else (ts > tq)
        masks.append(jnp.where(jnp.logical_or(earlier, jnp.logical_and(ri >= CHUNK, ts == tq)), 1.0, 0.0))
        lhs.append(jnp.concatenate([at_ref[...], rt_ref[...]], axis=0))
        rhs.append(jnp.concatenate([bt_ref[...], kt_ref[...]], axis=0))
        vv.append(v_ref[...])
        pc.append(pc_ref[...])
        s_all.append(s_ref[d])

    def head(arr, hd):
        return arr[:, hd * n:(hd + 1) * n]

    c = CHUNK
    bf = lambda t: t.astype(BF16)
    lhs_h = [bf(head(lhs[d], hd)) for d, hd in chains]
    rhs_h = [bf(head(rhs[d], hd)) for d, hd in chains]
    s_h = [head(s_all[d], hd) for d, hd in chains]
    v_h = [head(vv[d], hd) for d, hd in chains]
    gram = [_dot_nt(l_, r_) * masks[d] for (d, _), l_, r_ in zip(chains, lhs_h, rhs_h)]
    ls = [_dot_nt(l_, bf(si)) for l_, si in zip(lhs_h, s_h)]
    zeros = jnp.zeros((c, n), BF16)
    xin = [a[0:c] + _dot(bf(g[0:c]), jnp.concatenate([zeros, bf(vi)], axis=0))
           for a, g, vi in zip(ls, gram, v_h)]
    right = lax.broadcasted_iota(jnp.int32, (c, 2 * n), 1) >= n
    rj = [jnp.concatenate([g[0:c, 0:n], xi], axis=1) for g, xi in zip(gram, xin)]
    n_levels = c.bit_length() - 1
    for level in range(n_levels):
        nxt = []
        for r_ in rj:
            hi = bf(r_)
            p = _dot(hi[:, 0:n], hi)
            if level < n_levels - NEUMANN_SINGLE_PASS_LEVELS:
                lo = bf(r_ - hi.astype(F32))
                p = p + (_dot(hi[:, 0:n], lo) + _dot(lo[:, 0:n], hi))
            nxt.append(p + jnp.where(right, r_, 0.0))
        rj = nxt
    uv = [bf(jnp.concatenate([r_[:, n:], vi], axis=0)) for r_, vi in zip(rj, v_h)]
    y = [a[c:] + _dot(bf(g[c:]), uvi) for a, g, uvi in zip(ls, gram, uv)]
    s_new = [(si + _dot_tn(uvi, r_)) * head(pc[d], hd)
             for (d, hd), si, uvi, r_ in zip(chains, s_h, uv, rhs_h)]
    for d in range(2):
        y_refs[d][...] = jnp.concatenate([y[d * RWKV_HEADS + hd] for hd in range(RWKV_HEADS)], axis=1)
        s_ref[d] = jnp.concatenate([s_new[d * RWKV_HEADS + hd] for hd in range(RWKV_HEADS)], axis=1)


def _rwkv_scan_call(prep, batch, s_tot, ctx_len):
    v, _, _, rt0, at0, kt0, bt0, rt1, at1, kt1, bt1, pc = prep
    n_ch = s_tot // CHUNK
    n_ctx_ch = ctx_len // CHUNK
    per_tile = ROW_TILE // CHUNK
    c = RWKV_DIM

    def chunk_of(d, i):
        if d == 0:
            return i
        return jnp.where(i < n_ctx_ch, n_ctx_ch - 1 - i, n_ctx_ch + n_ch - 1 - i)

    def row_spec(d):
        return pl.BlockSpec((CHUNK, c), lambda b, i: (b * n_ch + chunk_of(d, i), 0))

    def pc_spec(d):
        def index(b, i):
            ch = b * n_ch + chunk_of(d, i)
            return ((ch // per_tile) * 2 * per_tile + d * per_tile + ch % per_tile, 0, 0)
        return pl.BlockSpec((None, 1, c), index)

    pc3 = pc.reshape(pc.shape[0], 1, c)
    ins, specs = [], []
    for d, arrs in enumerate(((rt0, at0, kt0, bt0), (rt1, at1, kt1, bt1))):
        ins += list(arrs) + [v, pc3]
        specs += [row_spec(d)] * 5 + [pc_spec(d)]
    return pl.pallas_call(
        _rwkv_scan_kernel,
        out_shape=[jax.ShapeDtypeStruct((batch * s_tot, c), F32)] * 2,
        grid=(batch, n_ch),
        in_specs=specs,
        out_specs=[row_spec(0), row_spec(1)],
        scratch_shapes=[pltpu.VMEM((2, RWKV_HEAD, c), F32)],
        compiler_params=_cparams(2),
        name="rwkv_scan",
    )(*ins)


def _merge_kernel(x_ref, g1_ref, sh_ref, sc_ref, gt_ref, a_ref, y0_ref, y1_ref, bonus_ref, rg_ref, m_ref,
                  lnw_ref, lnb_ref, bd_ref, wg_ref, wb_ref, wo_ref, o_ref):
    x = x_ref[...]
    h = (_rms(x, g1_ref[...]) * (1.0 + sc_ref[...]) + sh_ref[...]).astype(BF16)

    bd = bd_ref[...]
    y = y0_ref[...] + y1_ref[...]
    mu = _group_sum(y, bd) * (1.0 / RWKV_HEAD)
    yc = y - mu
    var = _group_sum(yc * yc, bd) * (1.0 / RWKV_HEAD)
    yn = yc * lax.rsqrt(var + LNX_EPS) * lnw_ref[...] + lnb_ref[...]
    r_out = ((yn + bonus_ref[...]) * rg_ref[...]).astype(BF16)

    d = D_MODEL
    mixed = None
    for i, yb in enumerate((a_ref[...], r_out, m_ref[...])):
        gate = jax.nn.sigmoid(_dot(h, wg_ref[:, i * d:(i + 1) * d]))
        term = gate * _dot(yb, wb_ref[i])
        mixed = term if mixed is None else mixed + term
    o_ref[...] = x + gt_ref[...] * _dot(mixed.astype(BF16), wo_ref[...])


def _merge_call(xs, mod, g1, a, y0, y1, bonus, rg, m, lnw, lnb, bd, wg, wb, wo, geom, latent_only):
    n_rows, n_tiles, n_ctx_tiles, ctx_row = geom
    c = RWKV_DIM
    if latent_only:
        n_lat = n_tiles - n_ctx_tiles
        n_out_tiles = (n_rows // ROW_TILE) // n_tiles * n_lat
        src_tile = lambda i: (i // n_lat) * n_tiles + n_ctx_tiles + i % n_lat
        ms = lambda which: _row_mod_spec(which, n_lat, 0, ctx_row)
    else:
        n_out_tiles = n_rows // ROW_TILE
        src_tile = lambda i: i
        ms = lambda which: _row_mod_spec(which, n_tiles, n_ctx_tiles, ctx_row)
    rows_in = lambda width: pl.BlockSpec((ROW_TILE, width), lambda i: (src_tile(i), 0))
    return pl.pallas_call(
        _merge_kernel,
        out_shape=jax.ShapeDtypeStruct((n_out_tiles * ROW_TILE, D_MODEL), F32),
        grid=(n_out_tiles,),
        in_specs=[
            rows_in(D_MODEL), _full((1, D_MODEL)), ms(0), ms(1), ms(2),
            rows_in(c), rows_in(c), rows_in(c), rows_in(c), rows_in(c), rows_in(c),
            _full(lnw.shape), _full(lnb.shape), _full(bd.shape),
            _full(wg.shape), _full(wb.shape), _full(wo.shape),
        ],
        out_specs=_rows(D_MODEL),
        compiler_params=_cparams(1),
        name="merge",
    )(xs, g1, mod, mod, mod, a, y0, y1, bonus, rg, m, lnw, lnb, bd, wg, wb, wo)


def _mlp_kernel(x_ref, g2_ref, sh_ref, sc_ref, gt_ref, w1_ref, w2_ref, gf_ref, o_ref, *, final_norm):
    x = x_ref[...]
    h = (_rms(x, g2_ref[...]) * (1.0 + sc_ref[...]) + sh_ref[...]).astype(BF16)
    acc = None
    blk = D_MODEL
    for j in range(D_FF // blk):
        a = jnp.maximum(_dot(h, w1_ref[:, j * blk:(j + 1) * blk]), 0.0)
        part = _dot((a * a).astype(BF16), w2_ref[j * blk:(j + 1) * blk, :])
        acc = part if acc is None else acc + part
    out = x + gt_ref[...] * acc
    if final_norm:
        out = _rms(out, gf_ref[...])
    o_ref[...] = out


def _mlp_call(xs, mod, g2, w1, w2, gf, geom, final_norm):
    n_rows, n_tiles, n_ctx_tiles, ctx_row = geom
    ms = lambda which: _row_mod_spec(which, n_tiles, n_ctx_tiles, ctx_row)
    return pl.pallas_call(
        functools.partial(_mlp_kernel, final_norm=final_norm),
        out_shape=jax.ShapeDtypeStruct((n_rows, D_MODEL), F32),
        grid=(n_rows // ROW_TILE,),
        in_specs=[
            _rows(D_MODEL), _full((1, D_MODEL)), ms(3), ms(4), ms(5),
            _full(w1.shape), _full(w2.shape), _full((1, D_MODEL)),
        ],
        out_specs=_rows(D_MODEL),
        compiler_params=_cparams(1),
        name="mlp",
    )(xs, g2, mod, mod, mod, w1, w2, gf)


def _rope_tables(ctx_len, seq):
    pos = jnp.arange(seq, dtype=jnp.int32)
    rows, cols = pos // GRID_W, pos % GRID_W

    def cs(p, n):
        inv = ROPE_THETA ** (-jnp.arange(n, dtype=F32) / n)
        ang = p.astype(F32)[:, None] * inv[None, :]
        return jnp.cos(ang), jnp.sin(ang)

    def axial(n):
        cr, sr = cs(rows, n)
        cc, sc = cs(cols, n)
        return jnp.concatenate([cr, cr, cc, cc], -1), jnp.concatenate([-sr, sr, -sc, sc], -1)

    def with_ctx(cos, sin):
        w = cos.shape[1]
        return (jnp.concatenate([jnp.ones((ctx_len, w), F32), cos], 0),
                jnp.concatenate([jnp.zeros((ctx_len, w), F32), sin], 0))

    gc, gs = axial(GQA_HEAD_DIM // 4)
    gc, gs = jnp.tile(gc, (1, 2)), jnp.tile(gs, (1, 2))
    mc, ms = axial(QK_ROPE // 4)
    pad = MLA_SLOT - QK_NOPE - QK_ROPE
    mc = jnp.concatenate([jnp.ones((seq, QK_NOPE), F32), mc, jnp.ones((seq, pad), F32)], -1)
    ms = jnp.concatenate([jnp.zeros((seq, QK_NOPE), F32), ms, jnp.zeros((seq, pad), F32)], -1)
    return with_ctx(gc, gs) + with_ctx(mc, ms)


def _block_ones(n, blk):
    i = jnp.arange(n)
    return (i[:, None] // blk == i[None, :] // blk)


def _chunk_tri():
    i = jnp.arange(ROW_TILE)
    same = _block_ones(ROW_TILE, CHUNK)
    fwd = jnp.logical_and(same, i[None, :] <= i[:, None])
    bwd = jnp.logical_and(same, i[None, :] >= i[:, None])
    return jnp.stack([fwd, bwd]).astype(BF16)


def _pack_layer(l, w_in, gqa_q_gain, gqa_k_gain, mla_q_up, mla_kv_up, rwkv_w2, rwkv_a2):
    w = w_in[l]
    o_q = 0
    o_rw = GQA_HEADS * GQA_HEAD_DIM + 2 * GQA_KV_HEADS * GQA_HEAD_DIM
    o_qd = o_rw + RWKV_IN
    o_kr = o_qd + Q_LORA + KV_LORA
    o_gate = o_kr + QK_ROPE
    zeros = lambda n: jnp.zeros((D_MODEL, n), F32)
    o_v = o_rw - GQA_KV_HEADS * GQA_HEAD_DIM
    w_v = jnp.pad(w[:, o_v:o_rw].reshape(D_MODEL, GQA_KV_HEADS, GQA_HEAD_DIM),
                  ((0, 0), (0, 0), (0, V_SLOT - GQA_HEAD_DIM))).reshape(D_MODEL, GQA_KV_HEADS * V_SLOT)
    w_main = jnp.concatenate(
        [w[:, o_q:o_v], w_v, w[:, o_rw:o_kr], zeros(QK_NOPE), w[:, o_kr:o_gate],
         zeros(MLA_SLOT - QK_NOPE - QK_ROPE)], axis=1)
    w_gate = w[:, o_gate:]
    gains = jnp.concatenate([jnp.tile(gqa_q_gain[l], GQA_HEADS), jnp.tile(gqa_k_gain[l], GQA_KV_HEADS)])[None, :]

    qup = mla_q_up[l].reshape(Q_LORA, MLA_HEADS, QK_NOPE + QK_ROPE)
    qup = jnp.pad(qup, ((0, 0), (0, 0), (0, MLA_SLOT - QK_NOPE - QK_ROPE))).reshape(Q_LORA, MLA_HEADS * MLA_SLOT)
    kvu = mla_kv_up[l].reshape(KV_LORA, MLA_HEADS, QK_NOPE + V_HEAD)
    kvk = jnp.pad(kvu[:, :, :QK_NOPE], ((0, 0), (0, 0), (0, MLA_SLOT - QK_NOPE))).reshape(KV_LORA, MLA_HEADS * MLA_SLOT)
    kvv = jnp.pad(kvu[:, :, QK_NOPE:], ((0, 0), (0, 0), (0, V_SLOT - V_HEAD))).reshape(KV_LORA, MLA_HEADS * V_SLOT)

    def pad_dir(w2):
        lora = w2.shape[1]
        z = jnp.zeros_like(w2[0])
        return jnp.stack([jnp.concatenate([w2[0], z], 0), jnp.concatenate([z, w2[1]], 0)]).astype(BF16)

    return dict(w_main=w_main.astype(BF16), w_gate=w_gate.astype(BF16), gains=gains,
                qup=qup.astype(BF16), kvk=kvk.astype(BF16), kvv=kvv.astype(BF16),
                w2p=pad_dir(rwkv_w2[l]), a2p=pad_dir(rwkv_a2[l]))


def kernel(x, c, ctx, c_ctx, w_mod, b_mod, g_norm1, g_norm2, w_in, gqa_q_gain, gqa_k_gain, rwkv_shift_mu, rwkv_w0, rwkv_w2, rwkv_a0, rwkv_a2, rwkv_g2, rwkv_k_k, rwkv_k_a, rwkv_r_k, rwkv_ln_w, rwkv_ln_b, mla_q_norm, mla_q_up, mla_kv_norm, mla_kv_up, w_branch, w_out, w_ff1, w_ff2, g_final):
    batch, seq, d = x.shape
    ctx_len = ctx.shape[1]
    depth = w_mod.shape[0]
    s_tot = ctx_len + seq
    assert d == D_MODEL and batch < 8
    assert ctx_len % ROW_TILE == 0 and seq % ROW_TILE == 0 and seq % GRID_W == 0
    n_rows = batch * s_tot
    geom = (n_rows, s_tot // ROW_TILE, ctx_len // ROW_TILE, batch)

    c_all = jnp.concatenate([c, c_ctx[None, :], jnp.zeros((8 - batch - 1, d), F32)], axis=0)
    mod_all = _mod_call(c_all, w_mod, b_mod).reshape(depth, 8, 6, 1, d)

    tabs = _rope_tables(ctx_len, seq)
    bd = _block_ones(LANES, RWKV_HEAD).astype(BF16)
    tri = _chunk_tri()
    row1 = lambda v: v.reshape(1, -1)

    xs = jnp.concatenate([ctx, x], axis=1).reshape(n_rows, d)
    for l in range(depth):
        pk = _pack_layer(l, w_in, gqa_q_gain, gqa_k_gain, mla_q_up, mla_kv_up, rwkv_w2, rwkv_a2)
        mod = mod_all[l]
        g1 = row1(g_norm1[l])
        gq, gk, gv, ru, mq, mk, mv = _inproj_call(
            xs, mod, g1, pk["w_main"], pk["gains"], bd, tabs,
            row1(mla_q_norm[l]), pk["qup"], row1(mla_kv_norm[l]), pk["kvk"], pk["kvv"], geom)
        a_out = _attn_call(gq, gk, gv, heads=GQA_HEADS, group=GQA_HEADS // GQA_KV_HEADS, dq=GQA_HEAD_DIM,
                           dv=GQA_HEAD_DIM, batch=batch, s_tot=s_tot, ctx_len=ctx_len, name="gqa_attn")
        m_out = _attn_call(mq, mk, mv, heads=MLA_HEADS, group=1, dq=MLA_SLOT, dv=V_HEAD,
                           batch=batch, s_tot=s_tot, ctx_len=ctx_len, name="mla_attn")
        prep = _rwkv_prep_call(ru, rwkv_shift_mu[l], row1(rwkv_k_k[l]), row1(rwkv_k_a[l]), row1(rwkv_r_k[l]),
                               rwkv_w0[l], pk["w2p"], rwkv_a0[l], pk["a2p"], rwkv_g2[l].astype(BF16), bd, tri, geom)
        y0, y1 = _rwkv_scan_call(prep, batch, s_tot, ctx_len)
        last = l == depth - 1
        xs = _merge_call(xs, mod, g1, a_out, y0, y1, prep[1], prep[2], m_out,
                         row1(rwkv_ln_w[l]), row1(rwkv_ln_b[l]), bd, pk["w_gate"],
                         w_branch[l].astype(BF16), w_out[l].astype(BF16), geom, latent_only=last)
        mlp_geom = (batch * seq, seq // ROW_TILE, 0, batch) if last else geom
        xs = _mlp_call(xs, mod, row1(g_norm2[l]), w_ff1[l].astype(BF16), w_ff2[l].astype(BF16),
                       row1(g_final), mlp_geom, final_norm=last)
    return xs.reshape(batch, seq, d)
```

```python
import functools

import jax
import jax.numpy as jnp
from jax import lax
from jax.experimental import pallas as pl
from jax.experimental.pallas import tpu as pltpu

F32 = jnp.float32
BF16 = jnp.bfloat16

D_MODEL = 1024
GRID_W = 64
ROPE_THETA = 10000.0
NORM_EPS = 1e-6

GQA_HEADS = 8
GQA_KV_HEADS = 2
GQA_HEAD_DIM = 64
GQA_SCALE = GQA_HEAD_DIM ** -0.5

RWKV_HEADS = 8
RWKV_HEAD = 64
RWKV_DIM = RWKV_HEADS * RWKV_HEAD
DECAY_LORA = 64
AAA_LORA = 64
GATE_LORA = 128
LNX_EPS = 64e-5
RWKV_IN = 3 * RWKV_DIM + 2 * DECAY_LORA + 2 * AAA_LORA + GATE_LORA

MLA_HEADS = 8
Q_LORA = 384
KV_LORA = 256
QK_NOPE = 64
QK_ROPE = 32
V_HEAD = 64
MLA_SCALE = (QK_NOPE + QK_ROPE) ** -0.5
MLA_SLOT = 128
ATTN_KEY_BLOCK = 512
LOG2_E = 1.4426950408889634

N_BRANCH = 3
BRANCH_W = 512
D_FF = 4 * D_MODEL

LANES = 128
ROW_TILE = 256
CHUNK = 64
NEUMANN_SINGLE_PASS_LEVELS = 1
VMEM_LIMIT = 56 * 1024 * 1024

_C_GQ = 0
_C_GK = _C_GQ + GQA_HEADS * GQA_HEAD_DIM
_C_GV = _C_GK + GQA_KV_HEADS * GQA_HEAD_DIM
_C_RW = _C_GV + GQA_KV_HEADS * GQA_HEAD_DIM
_C_QD = _C_RW + RWKV_IN
_C_KVD = _C_QD + Q_LORA
_C_KR = _C_KVD + KV_LORA
_C_END = _C_KR + LANES


def _cparams(n_axes):
    return pltpu.CompilerParams(dimension_semantics=("arbitrary",) * n_axes,
                                vmem_limit_bytes=VMEM_LIMIT)


def _dot(a, b):
    return jnp.dot(a, b, preferred_element_type=F32)


def _dot_nt(a, b):
    return lax.dot_general(a, b, (((1,), (1,)), ((), ())), preferred_element_type=F32)


def _dot_tn(a, b):
    return lax.dot_general(a, b, (((0,), (0,)), ((), ())), preferred_element_type=F32)


def _split(x):
    hi = x.astype(BF16)
    lo = (x - hi.astype(F32)).astype(BF16)
    return hi, lo


def _dot3(a, b, dot=_dot):
    m = a[0].shape[0]
    both = dot(jnp.concatenate([a[0], a[1]], axis=0), b[0])
    return both[0:m] + both[m:] + dot(a[0], b[1])


def _dot3_tn(a, b):
    m = a[0].shape[1]
    both = _dot_tn(jnp.concatenate([a[0], a[1]], axis=1), b[0])
    return both[0:m] + both[m:] + _dot_tn(a[0], b[1])


def _group_sum(x, bd):
    outs = []
    for g in range(x.shape[1] // LANES):
        hi, lo = _split(x[:, g * LANES:(g + 1) * LANES])
        outs.append(_dot(hi, bd) + _dot(lo, bd))
    return outs[0] if len(outs) == 1 else jnp.concatenate(outs, axis=1)


def _rope(x, cos, sin, half):
    lane = lax.broadcasted_iota(jnp.int32, x.shape, 1)
    up = pltpu.roll(x, LANES - half, axis=1)
    dn = pltpu.roll(x, half, axis=1)
    sw = jnp.where((lane % (2 * half)) < half, up, dn)
    return x * cos + sw * sin


def _rms(x, gain):
    ms = jnp.mean(x * x, axis=-1, keepdims=True)
    return x * lax.rsqrt(ms + NORM_EPS) * gain


def _mod_kernel(c_ref, w_ref, b_ref, o_ref):
    c = c_ref[...]
    s = (c * jax.nn.sigmoid(c)).astype(BF16)
    o_ref[...] = _dot(s, w_ref[...].astype(BF16)) + b_ref[...]


def _mod_call(c_all, w_mod, b_mod):
    n_layers, d, n = w_mod.shape
    bn = 1536
    return pl.pallas_call(
        _mod_kernel,
        out_shape=jax.ShapeDtypeStruct((n_layers, 8, n), F32),
        grid=(n_layers, n // bn),
        in_specs=[
            pl.BlockSpec((8, d), lambda l, j: (0, 0)),
            pl.BlockSpec((None, d, bn), lambda l, j: (l, 0, j)),
            pl.BlockSpec((None, 1, bn), lambda l, j: (l, 0, j)),
        ],
        out_specs=pl.BlockSpec((None, 8, bn), lambda l, j: (l, 0, j)),
        compiler_params=_cparams(2),
        name="adaln_mod",
    )(c_all, w_mod, b_mod.reshape(n_layers, 1, n))


def _inproj_kernel(x_ref, g1_ref, sh_ref, sc_ref, w_ref, gain_ref, bd_ref,
                   gc_ref, gs_ref, mc_ref, ms_ref,
                   qn_ref, qup_ref, kvn_ref, kvk_ref, kvv_ref,
                   gq_ref, gk_ref, gv_ref, ru_ref, mq_ref, mk_ref, mv_ref):
    x = x_ref[...]
    h = _rms(x, g1_ref[...]) * (1.0 + sc_ref[...]) + sh_ref[...]
    p = _dot(h.astype(BF16), w_ref[...])

    bd = bd_ref[...]
    gc, gs = gc_ref[...], gs_ref[...]
    n_q_slabs = GQA_HEADS * GQA_HEAD_DIM // LANES
    n_k_slabs = GQA_KV_HEADS * GQA_HEAD_DIM // LANES
    for g in range(n_q_slabs + n_k_slabs):
        slab = p[:, g * LANES:(g + 1) * LANES]
        ms = _group_sum(slab * slab, bd) * (1.0 / GQA_HEAD_DIM)
        y = slab * lax.rsqrt(ms + NORM_EPS) * gain_ref[:, g * LANES:(g + 1) * LANES]
        y = _rope(y, gc, gs, GQA_HEAD_DIM // 4)
        if g < n_q_slabs:
            gq_ref[:, g * LANES:(g + 1) * LANES] = (y * (GQA_SCALE * LOG2_E)).astype(BF16)
        else:
            gk_ref[:, (g - n_q_slabs) * LANES:(g - n_q_slabs + 1) * LANES] = y.astype(BF16)
    gv_ref[...] = p[:, _C_GV:_C_RW].astype(BF16)

    ru_ref[...] = p[:, _C_RW:_C_QD]

    mc, msn = mc_ref[...], ms_ref[...]
    qd = _rms(p[:, _C_QD:_C_KVD], qn_ref[...])
    q = _dot(qd.astype(BF16), qup_ref[...])
    kvd = _rms(p[:, _C_KVD:_C_KR], kvn_ref[...]).astype(BF16)
    kn = _dot(kvd, kvk_ref[...])
    kr = _rope(p[:, _C_KR:_C_END], mc, msn, QK_ROPE // 4)
    for hd in range(MLA_HEADS):
        sl = slice(hd * MLA_SLOT, (hd + 1) * MLA_SLOT)
        mq_ref[:, sl] = (_rope(q[:, sl], mc, msn, QK_ROPE // 4) * (MLA_SCALE * LOG2_E)).astype(BF16)
        mk_ref[:, sl] = (kn[:, sl] + kr).astype(BF16)
    mv_ref[...] = _dot(kvd, kvv_ref[...]).astype(BF16)


def _row_mod_spec(which, n_tiles, n_ctx_tiles, ctx_row):
    def index(i):
        b, t = i // n_tiles, i % n_tiles
        return (jnp.where(t < n_ctx_tiles, ctx_row, b), which, 0, 0)
    return pl.BlockSpec((None, None, 1, D_MODEL), index)


def _full(shape):
    return pl.BlockSpec(shape, lambda i: (0,) * len(shape))


def _rows(width):
    return pl.BlockSpec((ROW_TILE, width), lambda i: (i, 0))


def _inproj_call(xs, mod, g1, w_main, gains, bd, tabs, qn, qup, kvn, kvk, kvv, geom):
    n_rows, n_tiles, n_ctx_tiles, ctx_row = geom
    tab_spec = pl.BlockSpec((ROW_TILE, LANES), lambda i: (i % n_tiles, 0))
    widths = (GQA_HEADS * GQA_HEAD_DIM, GQA_KV_HEADS * GQA_HEAD_DIM, GQA_KV_HEADS * GQA_HEAD_DIM,
              RWKV_IN, MLA_HEADS * MLA_SLOT, MLA_HEADS * MLA_SLOT, MLA_HEADS * V_HEAD)
    dtypes = (BF16, BF16, BF16, F32, BF16, BF16, BF16)
    return pl.pallas_call(
        _inproj_kernel,
        out_shape=[jax.ShapeDtypeStruct((n_rows, w), dt) for w, dt in zip(widths, dtypes)],
        grid=(n_rows // ROW_TILE,),
        in_specs=[
            _rows(D_MODEL), _full((1, D_MODEL)),
            _row_mod_spec(0, n_tiles, n_ctx_tiles, ctx_row),
            _row_mod_spec(1, n_tiles, n_ctx_tiles, ctx_row),
            _full(w_main.shape), _full(gains.shape), _full(bd.shape),
            tab_spec, tab_spec, tab_spec, tab_spec,
            _full(qn.shape), _full(qup.shape), _full(kvn.shape), _full(kvk.shape), _full(kvv.shape),
        ],
        out_specs=[_rows(w) for w in widths],
        compiler_params=_cparams(1),
        name="in_proj",
    )(xs, g1, mod, mod, w_main, gains, bd, *tabs, qn, qup, kvn, kvk, kvv)


def _attn_kernel(q_ref, k_ref, vt_ref, o_ref, s_ref, *, heads, group, dq, dv, n_ctx_tiles, ctx_len, s_tot):
    t = pl.program_id(1)
    sub = 8

    def run(nk):
        blocks = [(lo, min(lo + ATTN_KEY_BLOCK, nk)) for lo in range(0, nk, ATTN_KEY_BLOCK)]

        def score_block(hq, lo, hi, m8):
            hk = hq // group
            sj = _dot_nt(k_ref[lo:hi, hk * dq:(hk + 1) * dq], q_ref[:, hq * dq:(hq + 1) * dq])
            s_ref[hq % 2, lo:hi, :] = sj
            mj = jnp.max(sj.reshape((hi - lo) // sub, sub, ROW_TILE), axis=0)
            return mj if m8 is None else jnp.maximum(m8, mj)

        def prob_block(hq, lo, hi, m, l8, acc):
            hk = hq // group
            e = jnp.exp2(s_ref[hq % 2, lo:hi, :] - m)
            lj = jnp.sum(e.reshape((hi - lo) // sub, sub, ROW_TILE), axis=0)
            pj = _dot(vt_ref[hk * dv:(hk + 1) * dv, lo:hi], e.astype(BF16))
            return (lj if l8 is None else l8 + lj), (pj if acc is None else acc + pj)

        m8 = None
        for lo, hi in blocks:
            m8 = score_block(0, lo, hi, m8)
        for hq in range(heads):
            m = jnp.max(m8, axis=0, keepdims=True)
            m8, l8, acc = None, None, None
            for lo, hi in blocks:
                if hq + 1 < heads:
                    m8 = score_block(hq + 1, lo, hi, m8)
                l8, acc = prob_block(hq, lo, hi, m, l8, acc)
            l = jnp.sum(l8, axis=0, keepdims=True)
            o_ref[hq * dv:(hq + 1) * dv, :] = (acc / l).astype(o_ref.dtype)

    @pl.when(t < n_ctx_tiles)
    def _():
        run(ctx_len)

    @pl.when(t >= n_ctx_tiles)
    def _():
        run(s_tot)


def _attn_call(q, k, v, *, heads, group, dq, dv, batch, s_tot, ctx_len, name):
    n_tiles = s_tot // ROW_TILE
    assert ctx_len % LANES == 0 and s_tot % LANES == 0
    vt = v.reshape(batch, s_tot, v.shape[1]).transpose(0, 2, 1)
    kern = functools.partial(_attn_kernel, heads=heads, group=group, dq=dq, dv=dv,
                             n_ctx_tiles=ctx_len // ROW_TILE, ctx_len=ctx_len, s_tot=s_tot)
    out_t = pl.pallas_call(
        kern,
        out_shape=jax.ShapeDtypeStruct((batch, heads * dv, s_tot), BF16),
        grid=(batch, n_tiles),
        in_specs=[
            pl.BlockSpec((ROW_TILE, q.shape[1]), lambda b, t: (b * n_tiles + t, 0)),
            pl.BlockSpec((s_tot, k.shape[1]), lambda b, t: (b, 0)),
            pl.BlockSpec((None, vt.shape[1], s_tot), lambda b, t: (b, 0, 0)),
        ],
        out_specs=pl.BlockSpec((None, heads * dv, ROW_TILE), lambda b, t: (b, 0, t)),
        scratch_shapes=[pltpu.VMEM((2, s_tot, ROW_TILE), F32)],
        compiler_params=_cparams(2),
        name=name,
    )(q, k, vt)
    return out_t.transpose(0, 2, 1).reshape(batch * s_tot, heads * dv)


def _rwkv_prep_kernel(u_ref, prev_ref, next_ref, mu_ref, kk_ref, ka_ref, rk_ref, w0_ref, w2_ref,
                      a0_ref, a2_ref, g2_ref, bd_ref, tri_ref,
                      v_ref, bonus_ref, gate_ref, rt0_ref, at0_ref, kt0_ref, bt0_ref,
                      rt1_ref, at1_ref, kt1_ref, bt1_ref, pc_ref, *, n_tiles, n_ctx_tiles):
    t = pl.program_id(0) % n_tiles
    u = u_ref[...]
    rows = u.shape[0]
    row = lax.broadcasted_iota(jnp.int32, u.shape, 0)
    seq_start = jnp.logical_or(t == 0, t == n_ctx_tiles)
    seq_end = jnp.logical_or(t == n_ctx_tiles - 1, t == n_tiles - 1)
    halo_prev = prev_ref[7:8, :] * jnp.where(seq_start, 0.0, 1.0)
    halo_next = next_ref[0:1, :] * jnp.where(seq_end, 0.0, 1.0)
    prev = jnp.where(row == 0, halo_prev, pltpu.roll(u, 1, axis=0))
    nxt = jnp.where(row == rows - 1, halo_next, pltpu.roll(u, rows - 1, axis=0))
    us = u + mu_ref[0:1, :] * (prev - u) + mu_ref[1:2, :] * (nxt - u)

    c = RWKV_DIM
    r, k, v = us[:, 0:c], us[:, c:2 * c], us[:, 2 * c:3 * c]
    wd = jnp.tanh(us[:, 3 * c:3 * c + LANES]).astype(BF16)
    ad = us[:, 3 * c + LANES:3 * c + 2 * LANES].astype(BF16)
    gd = us[:, 3 * c + 2 * LANES:3 * c + 3 * LANES]
    bd = bd_ref[...]

    kk = k * kk_ref[...]
    kk = kk / jnp.maximum(jnp.sqrt(_group_sum(kk * kk, bd)), 1e-12)

    v_ref[...] = v
    gate_ref[...] = _dot(jax.nn.sigmoid(gd).astype(BF16), g2_ref[...])

    outs = ((rt0_ref, at0_ref, kt0_ref, bt0_ref), (rt1_ref, at1_ref, kt1_ref, bt1_ref))
    n_chunks = rows // CHUNK
    bonus = None
    for d in range(2):
        z = w0_ref[d:d + 1, :] + _dot(wd, w2_ref[d])
        nz = -z
        softplus = jnp.maximum(nz, 0.0) + jnp.log(1.0 + jnp.exp(-jnp.abs(nz)))
        lw = -jnp.exp(-softplus - 0.5)
        a = jax.nn.sigmoid(a0_ref[d:d + 1, :] + _dot(ad, a2_ref[d]))
        k_d = k * (1.0 + (a - 1.0) * ka_ref[...])
        b_d = _group_sum(r * k_d * rk_ref[...], bd) * v
        bonus = b_d if bonus is None else bonus + b_d

        tri = tri_ref[d]
        hi = lw.astype(BF16)
        rem = lw - hi.astype(F32)
        mid = rem.astype(BF16)
        lo = (rem - mid.astype(F32)).astype(BF16)
        cum = _dot(tri, hi) + (_dot(tri, mid) + _dot(tri, lo))
        e_neg = jnp.exp(-cum)
        rt_ref, at_ref, kt_ref, bt_ref = outs[d]
        rt_ref[...] = r * jnp.exp(cum)
        at_ref[...] = -kk * jnp.exp(cum - lw)
        kt_ref[...] = k_d * e_neg
        bt_ref[...] = kk * a * e_neg
        last = CHUNK - 1 if d == 0 else 0
        tot = jnp.concatenate([cum[ci * CHUNK + last:ci * CHUNK + last + 1, :] for ci in range(n_chunks)], axis=0)
        pc_ref[d * n_chunks:(d + 1) * n_chunks, :] = jnp.exp(tot)
    bonus_ref[...] = bonus


def _rwkv_prep_call(ru, mu, kk, ka, rk, w0, w2p, a0, a2p, g2, bd, tri, geom):
    n_rows, n_tiles, n_ctx_tiles, _ = geom
    grid = n_rows // ROW_TILE
    g8 = ROW_TILE // 8
    last8 = n_rows // 8 - 1
    kern = functools.partial(_rwkv_prep_kernel, n_tiles=n_tiles, n_ctx_tiles=n_ctx_tiles)
    c = RWKV_DIM
    n_pc = 2 * (ROW_TILE // CHUNK)
    return pl.pallas_call(
        kern,
        out_shape=[jax.ShapeDtypeStruct((n_rows, c), F32)] * 11
                  + [jax.ShapeDtypeStruct((grid * n_pc, c), F32)],
        grid=(grid,),
        in_specs=[
            _rows(RWKV_IN),
            pl.BlockSpec((8, RWKV_IN), lambda i: (jnp.maximum(i * g8 - 1, 0), 0)),
            pl.BlockSpec((8, RWKV_IN), lambda i: (jnp.minimum((i + 1) * g8, last8), 0)),
            _full(mu.shape), _full(kk.shape), _full(ka.shape), _full(rk.shape),
            _full(w0.shape), _full(w2p.shape), _full(a0.shape), _full(a2p.shape),
            _full(g2.shape), _full(bd.shape), _full(tri.shape),
        ],
        out_specs=[_rows(c)] * 11 + [pl.BlockSpec((n_pc, c), lambda i: (i, 0))],
        compiler_params=_cparams(1),
        name="rwkv_prep",
    )(ru, ru, ru, mu, kk, ka, rk, w0, w2p, a0, a2p, g2, bd, tri)


def _rwkv_scan_kernel(*refs):
    ins, (y0_ref, y1_ref, s_ref) = refs[:12], refs[12:]
    y_refs = (y0_ref, y1_ref)

    @pl.when(pl.program_id(1) == 0)
    def _():
        s_ref[...] = jnp.zeros_like(s_ref)

    n = RWKV_HEAD
    ri = lax.broadcasted_iota(jnp.int32, (2 * CHUNK, 2 * CHUNK), 0)
    ci = lax.broadcasted_iota(jnp.int32, (2 * CHUNK, 2 * CHUNK), 1)
    tq, ts = ri % CHUNK, ci % CHUNK

    chains = [(d, hd) for d in range(2) for hd in range(RWKV_HEADS)]
    masks, lhs, rhs, vv, pc, s_all = [], [], [], [], [], []
    for d in range(2):
        rt_ref, at_ref, kt_ref, bt_ref, v_ref, pc_ref = ins[6 * d:6 * d + 6]
        earlier = (ts < tq) if d == 0 else (ts > tq)
        masks.append(jnp.where(jnp.logical_or(earlier, jnp.logical_and(ri >= CHUNK, ts == tq)), 1.0, 0.0))
        lhs.append(jnp.concatenate([at_ref[...], rt_ref[...]], axis=0))
        rhs.append(jnp.concatenate([bt_ref[...], kt_ref[...]], axis=0))
        vv.append(v_ref[...])
        pc.append(pc_ref[...])
        s_all.append(s_ref[d])

    def head(arr, hd):
        return arr[:, hd * n:(hd + 1) * n]

    c = CHUNK
    bf = lambda t: t.astype(BF16)
    lhs_h = [bf(head(lhs[d], hd)) for d, hd in chains]
    rhs_h = [bf(head(rhs[d], hd)) for d, hd in chains]
    s_h = [head(s_all[d], hd) for d, hd in chains]
    v_h = [head(vv[d], hd) for d, hd in chains]
    gram = [_dot_nt(l_, r_) * masks[d] for (d, _), l_, r_ in zip(chains, lhs_h, rhs_h)]
    ls = [_dot_nt(l_, bf(si)) for l_, si in zip(lhs_h, s_h)]
    zeros = jnp.zeros((c, n), BF16)
    xin = [a[0:c] + _dot(bf(g[0:c]), jnp.concatenate([zeros, bf(vi)], axis=0))
           for a, g, vi in zip(ls, gram, v_h)]
    right = lax.broadcasted_iota(jnp.int32, (c, 2 * n), 1) >= n
    rj = [jnp.concatenate([g[0:c, 0:n], xi], axis=1) for g, xi in zip(gram, xin)]
    n_levels = c.bit_length() - 1
    for level in range(n_levels):
        nxt = []
        for r_ in rj:
            hi = bf(r_)
            p = _dot(hi[:, 0:n], hi)
            if level < n_levels - NEUMANN_SINGLE_PASS_LEVELS:
                lo = bf(r_ - hi.astype(F32))
                p = p + (_dot(hi[:, 0:n], lo) + _dot(lo[:, 0:n], hi))
            nxt.append(p + jnp.where(right, r_, 0.0))
        rj = nxt
    uv = [bf(jnp.concatenate([r_[:, n:], vi], axis=0)) for r_, vi in zip(rj, v_h)]
    y = [a[c:] + _dot(bf(g[c:]), uvi) for a, g, uvi in zip(ls, gram, uv)]
    s_new = [(si + _dot_tn(uvi, r_)) * head(pc[d], hd)
             for (d, hd), si, uvi, r_ in zip(chains, s_h, uv, rhs_h)]
    for d in range(2):
        y_refs[d][...] = jnp.concatenate([y[d * RWKV_HEADS + hd] for hd in range(RWKV_HEADS)], axis=1)
        s_ref[d] = jnp.concatenate([s_new[d * RWKV_HEADS + hd] for hd in range(RWKV_HEADS)], axis=1)


def _rwkv_scan_call(prep, batch, s_tot, ctx_len):
    v, _, _, rt0, at0, kt0, bt0, rt1, at1, kt1, bt1, pc = prep
    n_ch = s_tot // CHUNK
    n_ctx_ch = ctx_len // CHUNK
    per_tile = ROW_TILE // CHUNK
    c = RWKV_DIM

    def chunk_of(d, i):
        if d == 0:
            return i
        return jnp.where(i < n_ctx_ch, n_ctx_ch - 1 - i, n_ctx_ch + n_ch - 1 - i)

    def row_spec(d):
        return pl.BlockSpec((CHUNK, c), lambda b, i: (b * n_ch + chunk_of(d, i), 0))

    def pc_spec(d):
        def index(b, i):
            ch = b * n_ch + chunk_of(d, i)
            return ((ch // per_tile) * 2 * per_tile + d * per_tile + ch % per_tile, 0, 0)
        return pl.BlockSpec((None, 1, c), index)

    pc3 = pc.reshape(pc.shape[0], 1, c)
    ins, specs = [], []
    for d, arrs in enumerate(((rt0, at0, kt0, bt0), (rt1, at1, kt1, bt1))):
        ins += list(arrs) + [v, pc3]
        specs += [row_spec(d)] * 5 + [pc_spec(d)]
    return pl.pallas_call(
        _rwkv_scan_kernel,
        out_shape=[jax.ShapeDtypeStruct((batch * s_tot, c), F32)] * 2,
        grid=(batch, n_ch),
        in_specs=specs,
        out_specs=[row_spec(0), row_spec(1)],
        scratch_shapes=[pltpu.VMEM((2, RWKV_HEAD, c), F32)],
        compiler_params=_cparams(2),
        name="rwkv_scan",
    )(*ins)


def _merge_kernel(x_ref, g1_ref, sh_ref, sc_ref, gt_ref, a_ref, y0_ref, y1_ref, bonus_ref, rg_ref, m_ref,
                  lnw_ref, lnb_ref, bd_ref, wg_ref, wb_ref, wo_ref, o_ref):
    x = x_ref[...]
    h = (_rms(x, g1_ref[...]) * (1.0 + sc_ref[...]) + sh_ref[...]).astype(BF16)

    bd = bd_ref[...]
    y = y0_ref[...] + y1_ref[...]
    mu = _group_sum(y, bd) * (1.0 / RWKV_HEAD)
    yc = y - mu
    var = _group_sum(yc * yc, bd) * (1.0 / RWKV_HEAD)
    yn = yc * lax.rsqrt(var + LNX_EPS) * lnw_ref[...] + lnb_ref[...]
    r_out = ((yn + bonus_ref[...]) * rg_ref[...]).astype(BF16)

    d = D_MODEL
    mixed = None
    for i, yb in enumerate((a_ref[...], r_out, m_ref[...])):
        gate = jax.nn.sigmoid(_dot(h, wg_ref[:, i * d:(i + 1) * d]))
        term = gate * _dot(yb, wb_ref[i])
        mixed = term if mixed is None else mixed + term
    o_ref[...] = x + gt_ref[...] * _dot(mixed.astype(BF16), wo_ref[...])


def _merge_call(xs, mod, g1, a, y0, y1, bonus, rg, m, lnw, lnb, bd, wg, wb, wo, geom, latent_only):
    n_rows, n_tiles, n_ctx_tiles, ctx_row = geom
    c = RWKV_DIM
    if latent_only:
        n_lat = n_tiles - n_ctx_tiles
        n_out_tiles = (n_rows // ROW_TILE) // n_tiles * n_lat
        src_tile = lambda i: (i // n_lat) * n_tiles + n_ctx_tiles + i % n_lat
        ms = lambda which: _row_mod_spec(which, n_lat, 0, ctx_row)
    else:
        n_out_tiles = n_rows // ROW_TILE
        src_tile = lambda i: i
        ms = lambda which: _row_mod_spec(which, n_tiles, n_ctx_tiles, ctx_row)
    rows_in = lambda width: pl.BlockSpec((ROW_TILE, width), lambda i: (src_tile(i), 0))
    return pl.pallas_call(
        _merge_kernel,
        out_shape=jax.ShapeDtypeStruct((n_out_tiles * ROW_TILE, D_MODEL), F32),
        grid=(n_out_tiles,),
        in_specs=[
            rows_in(D_MODEL), _full((1, D_MODEL)), ms(0), ms(1), ms(2),
            rows_in(c), rows_in(c), rows_in(c), rows_in(c), rows_in(c), rows_in(c),
            _full(lnw.shape), _full(lnb.shape), _full(bd.shape),
            _full(wg.shape), _full(wb.shape), _full(wo.shape),
        ],
        out_specs=_rows(D_MODEL),
        compiler_params=_cparams(1),
        name="merge",
    )(xs, g1, mod, mod, mod, a, y0, y1, bonus, rg, m, lnw, lnb, bd, wg, wb, wo)


def _mlp_kernel(x_ref, g2_ref, sh_ref, sc_ref, gt_ref, w1_ref, w2_ref, gf_ref, o_ref, *, final_norm):
    x = x_ref[...]
    h = (_rms(x, g2_ref[...]) * (1.0 + sc_ref[...]) + sh_ref[...]).astype(BF16)
    acc = None
    blk = D_MODEL
    for j in range(D_FF // blk):
        a = jnp.maximum(_dot(h, w1_ref[:, j * blk:(j + 1) * blk]), 0.0)
        part = _dot((a * a).astype(BF16), w2_ref[j * blk:(j + 1) * blk, :])
        acc = part if acc is None else acc + part
    out = x + gt_ref[...] * acc
    if final_norm:
        out = _rms(out, gf_ref[...])
    o_ref[...] = out


def _mlp_call(xs, mod, g2, w1, w2, gf, geom, final_norm):
    n_rows, n_tiles, n_ctx_tiles, ctx_row = geom
    ms = lambda which: _row_mod_spec(which, n_tiles, n_ctx_tiles, ctx_row)
    return pl.pallas_call(
        functools.partial(_mlp_kernel, final_norm=final_norm),
        out_shape=jax.ShapeDtypeStruct((n_rows, D_MODEL), F32),
        grid=(n_rows // ROW_TILE,),
        in_specs=[
            _rows(D_MODEL), _full((1, D_MODEL)), ms(3), ms(4), ms(5),
            _full(w1.shape), _full(w2.shape), _full((1, D_MODEL)),
        ],
        out_specs=_rows(D_MODEL),
        compiler_params=_cparams(1),
        name="mlp",
    )(xs, g2, mod, mod, mod, w1, w2, gf)


def _rope_tables(ctx_len, seq):
    pos = jnp.arange(seq, dtype=jnp.int32)
    rows, cols = pos // GRID_W, pos % GRID_W

    def cs(p, n):
        inv = ROPE_THETA ** (-jnp.arange(n, dtype=F32) / n)
        ang = p.astype(F32)[:, None] * inv[None, :]
        return jnp.cos(ang), jnp.sin(ang)

    def axial(n):
        cr, sr = cs(rows, n)
        cc, sc = cs(cols, n)
        return jnp.concatenate([cr, cr, cc, cc], -1), jnp.concatenate([-sr, sr, -sc, sc], -1)

    def with_ctx(cos, sin):
        w = cos.shape[1]
        return (jnp.concatenate([jnp.ones((ctx_len, w), F32), cos], 0),
                jnp.concatenate([jnp.zeros((ctx_len, w), F32), sin], 0))

    gc, gs = axial(GQA_HEAD_DIM // 4)
    gc, gs = jnp.tile(gc, (1, 2)), jnp.tile(gs, (1, 2))
    mc, ms = axial(QK_ROPE // 4)
    pad = MLA_SLOT - QK_NOPE - QK_ROPE
    mc = jnp.concatenate([jnp.ones((seq, QK_NOPE), F32), mc, jnp.ones((seq, pad), F32)], -1)
    ms = jnp.concatenate([jnp.zeros((seq, QK_NOPE), F32), ms, jnp.zeros((seq, pad), F32)], -1)
    return with_ctx(gc, gs) + with_ctx(mc, ms)


def _block_ones(n, blk):
    i = jnp.arange(n)
    return (i[:, None] // blk == i[None, :] // blk)


def _chunk_tri():
    i = jnp.arange(ROW_TILE)
    same = _block_ones(ROW_TILE, CHUNK)
    fwd = jnp.logical_and(same, i[None, :] <= i[:, None])
    bwd = jnp.logical_and(same, i[None, :] >= i[:, None])
    return jnp.stack([fwd, bwd]).astype(BF16)


def _pack_layer(l, w_in, gqa_q_gain, gqa_k_gain, mla_q_up, mla_kv_up, rwkv_w2, rwkv_a2):
    w = w_in[l]
    o_q = 0
    o_rw = GQA_HEADS * GQA_HEAD_DIM + 2 * GQA_KV_HEADS * GQA_HEAD_DIM
    o_qd = o_rw + RWKV_IN
    o_kr = o_qd + Q_LORA + KV_LORA
    o_gate = o_kr + QK_ROPE
    zeros = lambda n: jnp.zeros((D_MODEL, n), F32)
    w_main = jnp.concatenate(
        [w[:, o_q:o_kr], zeros(QK_NOPE), w[:, o_kr:o_gate], zeros(MLA_SLOT - QK_NOPE - QK_ROPE)], axis=1)
    w_gate = w[:, o_gate:]
    gains = jnp.concatenate([jnp.tile(gqa_q_gain[l], GQA_HEADS), jnp.tile(gqa_k_gain[l], GQA_KV_HEADS)])[None, :]

    qup = mla_q_up[l].reshape(Q_LORA, MLA_HEADS, QK_NOPE + QK_ROPE)
    qup = jnp.pad(qup, ((0, 0), (0, 0), (0, MLA_SLOT - QK_NOPE - QK_ROPE))).reshape(Q_LORA, MLA_HEADS * MLA_SLOT)
    kvu = mla_kv_up[l].reshape(KV_LORA, MLA_HEADS, QK_NOPE + V_HEAD)
    kvk = jnp.pad(kvu[:, :, :QK_NOPE], ((0, 0), (0, 0), (0, MLA_SLOT - QK_NOPE))).reshape(KV_LORA, MLA_HEADS * MLA_SLOT)
    kvv = kvu[:, :, QK_NOPE:].reshape(KV_LORA, MLA_HEADS * V_HEAD)

    def pad_dir(w2):
        lora = w2.shape[1]
        z = jnp.zeros_like(w2[0])
        return jnp.stack([jnp.concatenate([w2[0], z], 0), jnp.concatenate([z, w2[1]], 0)]).astype(BF16)

    return dict(w_main=w_main.astype(BF16), w_gate=w_gate.astype(BF16), gains=gains,
                qup=qup.astype(BF16), kvk=kvk.astype(BF16), kvv=kvv.astype(BF16),
                w2p=pad_dir(rwkv_w2[l]), a2p=pad_dir(rwkv_a2[l]))


def kernel(x, c, ctx, c_ctx, w_mod, b_mod, g_norm1, g_norm2, w_in, gqa_q_gain, gqa_k_gain, rwkv_shift_mu, rwkv_w0, rwkv_w2, rwkv_a0, rwkv_a2, rwkv_g2, rwkv_k_k, rwkv_k_a, rwkv_r_k, rwkv_ln_w, rwkv_ln_b, mla_q_norm, mla_q_up, mla_kv_norm, mla_kv_up, w_branch, w_out, w_ff1, w_ff2, g_final):
    batch, seq, d = x.shape
    ctx_len = ctx.shape[1]
    depth = w_mod.shape[0]
    s_tot = ctx_len + seq
    assert d == D_MODEL and batch < 8
    assert ctx_len % ROW_TILE == 0 and seq % ROW_TILE == 0 and seq % GRID_W == 0
    n_rows = batch * s_tot
    geom = (n_rows, s_tot // ROW_TILE, ctx_len // ROW_TILE, batch)

    c_all = jnp.concatenate([c, c_ctx[None, :], jnp.zeros((8 - batch - 1, d), F32)], axis=0)
    mod_all = _mod_call(c_all, w_mod, b_mod).reshape(depth, 8, 6, 1, d)

    tabs = _rope_tables(ctx_len, seq)
    bd = _block_ones(LANES, RWKV_HEAD).astype(BF16)
    tri = _chunk_tri()
    row1 = lambda v: v.reshape(1, -1)

    xs = jnp.concatenate([ctx, x], axis=1).reshape(n_rows, d)
    for l in range(depth):
        pk = _pack_layer(l, w_in, gqa_q_gain, gqa_k_gain, mla_q_up, mla_kv_up, rwkv_w2, rwkv_a2)
        mod = mod_all[l]
        g1 = row1(g_norm1[l])
        gq, gk, gv, ru, mq, mk, mv = _inproj_call(
            xs, mod, g1, pk["w_main"], pk["gains"], bd, tabs,
            row1(mla_q_norm[l]), pk["qup"], row1(mla_kv_norm[l]), pk["kvk"], pk["kvv"], geom)
        a_out = _attn_call(gq, gk, gv, heads=GQA_HEADS, group=GQA_HEADS // GQA_KV_HEADS, dq=GQA_HEAD_DIM,
                           dv=GQA_HEAD_DIM, batch=batch, s_tot=s_tot, ctx_len=ctx_len, name="gqa_attn")
        m_out = _attn_call(mq, mk, mv, heads=MLA_HEADS, group=1, dq=MLA_SLOT, dv=V_HEAD,
                           batch=batch, s_tot=s_tot, ctx_len=ctx_len, name="mla_attn")
        prep = _rwkv_prep_call(ru, rwkv_shift_mu[l], row1(rwkv_k_k[l]), row1(rwkv_k_a[l]), row1(rwkv_r_k[l]),
                               rwkv_w0[l], pk["w2p"], rwkv_a0[l], pk["a2p"], rwkv_g2[l].astype(BF16), bd, tri, geom)
        y0, y1 = _rwkv_scan_call(prep, batch, s_tot, ctx_len)
        last = l == depth - 1
        xs = _merge_call(xs, mod, g1, a_out, y0, y1, prep[1], prep[2], m_out,
                         row1(rwkv_ln_w[l]), row1(rwkv_ln_b[l]), bd, pk["w_gate"],
                         w_branch[l].astype(BF16), w_out[l].astype(BF16), geom, latent_only=last)
        mlp_geom = (batch * seq, seq // ROW_TILE, 0, batch) if last else geom
        xs = _mlp_call(xs, mod, row1(g_norm2[l]), w_ff1[l].astype(BF16), w_ff2[l].astype(BF16),
                       row1(g_final), mlp_geom, final_norm=last)
    return xs.reshape(batch, seq, d)
```

```python
import functools

import jax
import jax.numpy as jnp
from jax import lax
from jax.experimental import pallas as pl
from jax.experimental.pallas import tpu as pltpu

F32 = jnp.float32
BF16 = jnp.bfloat16

D_MODEL = 1024
GRID_W = 64
ROPE_THETA = 10000.0
NORM_EPS = 1e-6

GQA_HEADS = 8
GQA_KV_HEADS = 2
GQA_HEAD_DIM = 64
GQA_SCALE = GQA_HEAD_DIM ** -0.5

RWKV_HEADS = 8
RWKV_HEAD = 64
RWKV_DIM = RWKV_HEADS * RWKV_HEAD
DECAY_LORA = 64
AAA_LORA = 64
GATE_LORA = 128
LNX_EPS = 64e-5
RWKV_IN = 3 * RWKV_DIM + 2 * DECAY_LORA + 2 * AAA_LORA + GATE_LORA

MLA_HEADS = 8
Q_LORA = 384
KV_LORA = 256
QK_NOPE = 64
QK_ROPE = 32
V_HEAD = 64
MLA_SCALE = (QK_NOPE + QK_ROPE) ** -0.5
MLA_SLOT = 128
ATTN_KEY_BLOCK = 512
LOG2_E = 1.4426950408889634

N_BRANCH = 3
BRANCH_W = 512
D_FF = 4 * D_MODEL

LANES = 128
ROW_TILE = 256
CHUNK = 64
SCAN_BATCH = 2
NEUMANN_SINGLE_PASS_LEVELS = 1
VMEM_LIMIT = 56 * 1024 * 1024

_C_GQ = 0
_C_GK = _C_GQ + GQA_HEADS * GQA_HEAD_DIM
_C_GV = _C_GK + GQA_KV_HEADS * GQA_HEAD_DIM
_C_RW = _C_GV + GQA_KV_HEADS * GQA_HEAD_DIM
_C_QD = _C_RW + RWKV_IN
_C_KVD = _C_QD + Q_LORA
_C_KR = _C_KVD + KV_LORA
_C_END = _C_KR + LANES


def _cparams(n_axes):
    return pltpu.CompilerParams(dimension_semantics=("arbitrary",) * n_axes,
                                vmem_limit_bytes=VMEM_LIMIT)


def _dot(a, b):
    return jnp.dot(a, b, preferred_element_type=F32)


def _dot_nt(a, b):
    return lax.dot_general(a, b, (((1,), (1,)), ((), ())), preferred_element_type=F32)


def _dot_tn(a, b):
    return lax.dot_general(a, b, (((0,), (0,)), ((), ())), preferred_element_type=F32)


def _split(x):
    hi = x.astype(BF16)
    lo = (x - hi.astype(F32)).astype(BF16)
    return hi, lo


def _dot3(a, b, dot=_dot):
    m = a[0].shape[0]
    both = dot(jnp.concatenate([a[0], a[1]], axis=0), b[0])
    return both[0:m] + both[m:] + dot(a[0], b[1])


def _dot3_tn(a, b):
    m = a[0].shape[1]
    both = _dot_tn(jnp.concatenate([a[0], a[1]], axis=1), b[0])
    return both[0:m] + both[m:] + _dot_tn(a[0], b[1])


def _group_sum(x, bd):
    outs = []
    for g in range(x.shape[1] // LANES):
        hi, lo = _split(x[:, g * LANES:(g + 1) * LANES])
        outs.append(_dot(hi, bd) + _dot(lo, bd))
    return outs[0] if len(outs) == 1 else jnp.concatenate(outs, axis=1)


def _rope(x, cos, sin, half):
    lane = lax.broadcasted_iota(jnp.int32, x.shape, 1)
    up = pltpu.roll(x, LANES - half, axis=1)
    dn = pltpu.roll(x, half, axis=1)
    sw = jnp.where((lane % (2 * half)) < half, up, dn)
    return x * cos + sw * sin


def _rms(x, gain):
    ms = jnp.mean(x * x, axis=-1, keepdims=True)
    return x * lax.rsqrt(ms + NORM_EPS) * gain


def _mod_kernel(c_ref, w_ref, b_ref, o_ref):
    c = c_ref[...]
    s = (c * jax.nn.sigmoid(c)).astype(BF16)
    o_ref[...] = _dot(s, w_ref[...].astype(BF16)) + b_ref[...]


def _mod_call(c_all, w_mod, b_mod):
    n_layers, d, n = w_mod.shape
    bn = 1536
    return pl.pallas_call(
        _mod_kernel,
        out_shape=jax.ShapeDtypeStruct((n_layers, 8, n), F32),
        grid=(n_layers, n // bn),
        in_specs=[
            pl.BlockSpec((8, d), lambda l, j: (0, 0)),
            pl.BlockSpec((None, d, bn), lambda l, j: (l, 0, j)),
            pl.BlockSpec((None, 1, bn), lambda l, j: (l, 0, j)),
        ],
        out_specs=pl.BlockSpec((None, 8, bn), lambda l, j: (l, 0, j)),
        compiler_params=_cparams(2),
        name="adaln_mod",
    )(c_all, w_mod, b_mod.reshape(n_layers, 1, n))


def _inproj_kernel(x_ref, g1_ref, sh_ref, sc_ref, w_ref, gain_ref, bd_ref,
                   gc_ref, gs_ref, mc_ref, ms_ref,
                   qn_ref, qup_ref, kvn_ref, kvk_ref, kvv_ref,
                   gq_ref, gk_ref, gv_ref, ru_ref, mq_ref, mk_ref, mv_ref):
    x = x_ref[...]
    h = _rms(x, g1_ref[...]) * (1.0 + sc_ref[...]) + sh_ref[...]
    p = _dot(h.astype(BF16), w_ref[...])

    bd = bd_ref[...]
    gc, gs = gc_ref[...], gs_ref[...]
    n_q_slabs = GQA_HEADS * GQA_HEAD_DIM // LANES
    n_k_slabs = GQA_KV_HEADS * GQA_HEAD_DIM // LANES
    for g in range(n_q_slabs + n_k_slabs):
        slab = p[:, g * LANES:(g + 1) * LANES]
        ms = _group_sum(slab * slab, bd) * (1.0 / GQA_HEAD_DIM)
        y = slab * lax.rsqrt(ms + NORM_EPS) * gain_ref[:, g * LANES:(g + 1) * LANES]
        y = _rope(y, gc, gs, GQA_HEAD_DIM // 4)
        if g < n_q_slabs:
            gq_ref[:, g * LANES:(g + 1) * LANES] = (y * (GQA_SCALE * LOG2_E)).astype(BF16)
        else:
            gk_ref[:, (g - n_q_slabs) * LANES:(g - n_q_slabs + 1) * LANES] = y.astype(BF16)
    gv_ref[...] = p[:, _C_GV:_C_RW].astype(BF16)

    ru_ref[...] = p[:, _C_RW:_C_QD]

    mc, msn = mc_ref[...], ms_ref[...]
    qd = _rms(p[:, _C_QD:_C_KVD], qn_ref[...])
    q = _dot(qd.astype(BF16), qup_ref[...])
    kvd = _rms(p[:, _C_KVD:_C_KR], kvn_ref[...]).astype(BF16)
    kn = _dot(kvd, kvk_ref[...])
    kr = _rope(p[:, _C_KR:_C_END], mc, msn, QK_ROPE // 4)
    for hd in range(MLA_HEADS):
        sl = slice(hd * MLA_SLOT, (hd + 1) * MLA_SLOT)
        mq_ref[:, sl] = (_rope(q[:, sl], mc, msn, QK_ROPE // 4) * (MLA_SCALE * LOG2_E)).astype(BF16)
        mk_ref[:, sl] = (kn[:, sl] + kr).astype(BF16)
    mv_ref[...] = _dot(kvd, kvv_ref[...]).astype(BF16)


def _row_mod_spec(which, n_tiles, n_ctx_tiles, ctx_row):
    def index(i):
        b, t = i // n_tiles, i % n_tiles
        return (jnp.where(t < n_ctx_tiles, ctx_row, b), which, 0, 0)
    return pl.BlockSpec((None, None, 1, D_MODEL), index)


def _full(shape):
    return pl.BlockSpec(shape, lambda i: (0,) * len(shape))


def _rows(width):
    return pl.BlockSpec((ROW_TILE, width), lambda i: (i, 0))


def _inproj_call(xs, mod, g1, w_main, gains, bd, tabs, qn, qup, kvn, kvk, kvv, geom):
    n_rows, n_tiles, n_ctx_tiles, ctx_row = geom
    tab_spec = pl.BlockSpec((ROW_TILE, LANES), lambda i: (i % n_tiles, 0))
    widths = (GQA_HEADS * GQA_HEAD_DIM, GQA_KV_HEADS * GQA_HEAD_DIM, GQA_KV_HEADS * GQA_HEAD_DIM,
              RWKV_IN, MLA_HEADS * MLA_SLOT, MLA_HEADS * MLA_SLOT, MLA_HEADS * V_HEAD)
    dtypes = (BF16, BF16, BF16, F32, BF16, BF16, BF16)
    return pl.pallas_call(
        _inproj_kernel,
        out_shape=[jax.ShapeDtypeStruct((n_rows, w), dt) for w, dt in zip(widths, dtypes)],
        grid=(n_rows // ROW_TILE,),
        in_specs=[
            _rows(D_MODEL), _full((1, D_MODEL)),
            _row_mod_spec(0, n_tiles, n_ctx_tiles, ctx_row),
            _row_mod_spec(1, n_tiles, n_ctx_tiles, ctx_row),
            _full(w_main.shape), _full(gains.shape), _full(bd.shape),
            tab_spec, tab_spec, tab_spec, tab_spec,
            _full(qn.shape), _full(qup.shape), _full(kvn.shape), _full(kvk.shape), _full(kvv.shape),
        ],
        out_specs=[_rows(w) for w in widths],
        compiler_params=_cparams(1),
        name="in_proj",
    )(xs, g1, mod, mod, w_main, gains, bd, *tabs, qn, qup, kvn, kvk, kvv)


def _attn_kernel(q_ref, k_ref, vt_ref, o_ref, s_ref, *, heads, group, dq, dv, n_ctx_tiles, ctx_len, s_tot):
    t = pl.program_id(1)
    sub = 8

    def run(nk):
        blocks = [(lo, min(lo + ATTN_KEY_BLOCK, nk)) for lo in range(0, nk, ATTN_KEY_BLOCK)]

        def score_block(hq, lo, hi, m8):
            hk = hq // group
            sj = _dot_nt(k_ref[lo:hi, hk * dq:(hk + 1) * dq], q_ref[:, hq * dq:(hq + 1) * dq])
            s_ref[hq % 2, lo:hi, :] = sj
            mj = jnp.max(sj.reshape((hi - lo) // sub, sub, ROW_TILE), axis=0)
            return mj if m8 is None else jnp.maximum(m8, mj)

        def prob_block(hq, lo, hi, m, l8, acc):
            hk = hq // group
            e = jnp.exp2(s_ref[hq % 2, lo:hi, :] - m)
            lj = jnp.sum(e.reshape((hi - lo) // sub, sub, ROW_TILE), axis=0)
            pj = _dot(vt_ref[hk * dv:(hk + 1) * dv, lo:hi], e.astype(BF16))
            return (lj if l8 is None else l8 + lj), (pj if acc is None else acc + pj)

        m8 = None
        for lo, hi in blocks:
            m8 = score_block(0, lo, hi, m8)
        for hq in range(heads):
            m = jnp.max(m8, axis=0, keepdims=True)
            m8, l8, acc = None, None, None
            for lo, hi in blocks:
                if hq + 1 < heads:
                    m8 = score_block(hq + 1, lo, hi, m8)
                l8, acc = prob_block(hq, lo, hi, m, l8, acc)
            l = jnp.sum(l8, axis=0, keepdims=True)
            o_ref[hq * dv:(hq + 1) * dv, :] = (acc / l).astype(o_ref.dtype)

    @pl.when(t < n_ctx_tiles)
    def _():
        run(ctx_len)

    @pl.when(t >= n_ctx_tiles)
    def _():
        run(s_tot)


def _attn_call(q, k, v, *, heads, group, dq, dv, batch, s_tot, ctx_len, name):
    n_tiles = s_tot // ROW_TILE
    assert ctx_len % LANES == 0 and s_tot % LANES == 0
    vt = v.reshape(batch, s_tot, v.shape[1]).transpose(0, 2, 1)
    kern = functools.partial(_attn_kernel, heads=heads, group=group, dq=dq, dv=dv,
                             n_ctx_tiles=ctx_len // ROW_TILE, ctx_len=ctx_len, s_tot=s_tot)
    out_t = pl.pallas_call(
        kern,
        out_shape=jax.ShapeDtypeStruct((batch, heads * dv, s_tot), BF16),
        grid=(batch, n_tiles),
        in_specs=[
            pl.BlockSpec((ROW_TILE, q.shape[1]), lambda b, t: (b * n_tiles + t, 0)),
            pl.BlockSpec((s_tot, k.shape[1]), lambda b, t: (b, 0)),
            pl.BlockSpec((None, vt.shape[1], s_tot), lambda b, t: (b, 0, 0)),
        ],
        out_specs=pl.BlockSpec((None, heads * dv, ROW_TILE), lambda b, t: (b, 0, t)),
        scratch_shapes=[pltpu.VMEM((2, s_tot, ROW_TILE), F32)],
        compiler_params=_cparams(2),
        name=name,
    )(q, k, vt)
    return out_t.transpose(0, 2, 1).reshape(batch * s_tot, heads * dv)


def _rwkv_prep_kernel(u_ref, prev_ref, next_ref, mu_ref, kk_ref, ka_ref, rk_ref, w0_ref, w2_ref,
                      a0_ref, a2_ref, g2_ref, bd_ref, tri_ref,
                      v_ref, bonus_ref, gate_ref, lhs0_ref, rhs0_ref, lhs1_ref, rhs1_ref, pc_ref,
                      *, n_tiles, n_ctx_tiles):
    t = pl.program_id(0) % n_tiles
    u = u_ref[...]
    rows = u.shape[0]
    row = lax.broadcasted_iota(jnp.int32, u.shape, 0)
    seq_start = jnp.logical_or(t == 0, t == n_ctx_tiles)
    seq_end = jnp.logical_or(t == n_ctx_tiles - 1, t == n_tiles - 1)
    halo_prev = prev_ref[7:8, :] * jnp.where(seq_start, 0.0, 1.0)
    halo_next = next_ref[0:1, :] * jnp.where(seq_end, 0.0, 1.0)
    prev = jnp.where(row == 0, halo_prev, pltpu.roll(u, 1, axis=0))
    nxt = jnp.where(row == rows - 1, halo_next, pltpu.roll(u, rows - 1, axis=0))
    us = u + mu_ref[0:1, :] * (prev - u) + mu_ref[1:2, :] * (nxt - u)

    c = RWKV_DIM
    r, k, v = us[:, 0:c], us[:, c:2 * c], us[:, 2 * c:3 * c]
    wd = jnp.tanh(us[:, 3 * c:3 * c + LANES]).astype(BF16)
    ad = us[:, 3 * c + LANES:3 * c + 2 * LANES].astype(BF16)
    gd = us[:, 3 * c + 2 * LANES:3 * c + 3 * LANES]
    bd = bd_ref[...]

    kk = k * kk_ref[...]
    kk = kk / jnp.maximum(jnp.sqrt(_group_sum(kk * kk, bd)), 1e-12)

    v_ref[...] = v.astype(BF16)
    gate_ref[...] = _dot(jax.nn.sigmoid(gd).astype(BF16), g2_ref[...])

    outs = ((lhs0_ref, rhs0_ref), (lhs1_ref, rhs1_ref))
    n_chunks = rows // CHUNK
    bonus = None
    for d in range(2):
        z = w0_ref[d:d + 1, :] + _dot(wd, w2_ref[d])
        nz = -z
        softplus = jnp.maximum(nz, 0.0) + jnp.log(1.0 + jnp.exp(-jnp.abs(nz)))
        lw = -jnp.exp(-softplus - 0.5)
        a = jax.nn.sigmoid(a0_ref[d:d + 1, :] + _dot(ad, a2_ref[d]))
        k_d = k * (1.0 + (a - 1.0) * ka_ref[...])
        b_d = _group_sum(r * k_d * rk_ref[...], bd) * v
        bonus = b_d if bonus is None else bonus + b_d

        tri = tri_ref[d]
        hi = lw.astype(BF16)
        rem = lw - hi.astype(F32)
        mid = rem.astype(BF16)
        lo = (rem - mid.astype(F32)).astype(BF16)
        cum = _dot(tri, hi) + (_dot(tri, mid) + _dot(tri, lo))
        e_neg = jnp.exp(-cum)
        lhs_ref, rhs_ref = outs[d]
        rt = (r * jnp.exp(cum)).astype(BF16)
        at = (-kk * jnp.exp(cum - lw)).astype(BF16)
        kt = (k_d * e_neg).astype(BF16)
        bt = (kk * a * e_neg).astype(BF16)
        for ci in range(n_chunks):
            rs = slice(ci * CHUNK, (ci + 1) * CHUNK)
            lhs_ref[ci, 0:CHUNK, :] = at[rs]
            lhs_ref[ci, CHUNK:, :] = rt[rs]
            rhs_ref[ci, 0:CHUNK, :] = bt[rs]
            rhs_ref[ci, CHUNK:, :] = kt[rs]
        last = CHUNK - 1 if d == 0 else 0
        tot = jnp.concatenate([cum[ci * CHUNK + last:ci * CHUNK + last + 1, :] for ci in range(n_chunks)], axis=0)
        pc_ref[d * n_chunks:(d + 1) * n_chunks, :] = jnp.exp(tot)
    bonus_ref[...] = bonus


def _rwkv_prep_call(ru, mu, kk, ka, rk, w0, w2p, a0, a2p, g2, bd, tri, geom):
    n_rows, n_tiles, n_ctx_tiles, _ = geom
    grid = n_rows // ROW_TILE
    g8 = ROW_TILE // 8
    last8 = n_rows // 8 - 1
    kern = functools.partial(_rwkv_prep_kernel, n_tiles=n_tiles, n_ctx_tiles=n_ctx_tiles)
    c = RWKV_DIM
    n_pc = 2 * (ROW_TILE // CHUNK)
    return pl.pallas_call(
        kern,
        out_shape=[jax.ShapeDtypeStruct((n_rows, c), BF16)] + [jax.ShapeDtypeStruct((n_rows, c), F32)] * 2
                  + [jax.ShapeDtypeStruct((n_rows // CHUNK, 2 * CHUNK, c), BF16)] * 4
                  + [jax.ShapeDtypeStruct((grid * n_pc, c), F32)],
        grid=(grid,),
        in_specs=[
            _rows(RWKV_IN),
            pl.BlockSpec((8, RWKV_IN), lambda i: (jnp.maximum(i * g8 - 1, 0), 0)),
            pl.BlockSpec((8, RWKV_IN), lambda i: (jnp.minimum((i + 1) * g8, last8), 0)),
            _full(mu.shape), _full(kk.shape), _full(ka.shape), _full(rk.shape),
            _full(w0.shape), _full(w2p.shape), _full(a0.shape), _full(a2p.shape),
            _full(g2.shape), _full(bd.shape), _full(tri.shape),
        ],
        out_specs=[_rows(c)] * 3
                  + [pl.BlockSpec((ROW_TILE // CHUNK, 2 * CHUNK, c), lambda i: (i, 0, 0))] * 4
                  + [pl.BlockSpec((n_pc, c), lambda i: (i, 0))],
        compiler_params=_cparams(1),
        name="rwkv_prep",
    )(ru, ru, ru, mu, kk, ka, rk, w0, w2p, a0, a2p, g2, bd, tri)


def _rwkv_scan_kernel(*refs, nb):
    ins, (y0_ref, y1_ref, s_ref) = refs[:8], refs[8:]
    y_refs = (y0_ref, y1_ref)

    @pl.when(pl.program_id(1) == 0)
    def _():
        s_ref[...] = jnp.zeros_like(s_ref)

    n = RWKV_HEAD
    c = CHUNK
    ri = lax.broadcasted_iota(jnp.int32, (2 * c, 2 * c), 0)
    ci = lax.broadcasted_iota(jnp.int32, (2 * c, 2 * c), 1)
    tq, ts = ri % c, ci % c
    masks = []
    for d in range(2):
        earlier = (ts < tq) if d == 0 else (ts > tq)
        masks.append(jnp.where(jnp.logical_or(earlier, jnp.logical_and(ri >= c, ts == tq)), 1.0, 0.0))

    chains = [(b, d, hd) for b in range(nb) for d in range(2) for hd in range(RWKV_HEADS)]
    bf = lambda t: t.astype(BF16)

    def head(arr, hd):
        return arr[:, hd * n:(hd + 1) * n]

    lhs_h = [head(ins[4 * d][b], hd) for b, d, hd in chains]
    rhs_h = [head(ins[4 * d + 1][b], hd) for b, d, hd in chains]
    v_h = [head(ins[4 * d + 2][b], hd) for b, d, hd in chains]
    pc_h = [head(ins[4 * d + 3][b], hd) for b, d, hd in chains]
    s_h = [head(s_ref[b, d], hd) for b, d, hd in chains]
    gram = [_dot_nt(l_, r_) * masks[d] for (_, d, _), l_, r_ in zip(chains, lhs_h, rhs_h)]
    ls = [_dot_nt(l_, bf(si)) for l_, si in zip(lhs_h, s_h)]
    zeros = jnp.zeros((c, n), BF16)
    xin = [a[0:c] + _dot(bf(g[0:c]), jnp.concatenate([zeros, vi], axis=0))
           for a, g, vi in zip(ls, gram, v_h)]
    right = lax.broadcasted_iota(jnp.int32, (c, 2 * n), 1) >= n
    rj = [jnp.concatenate([g[0:c, 0:n], xi], axis=1) for g, xi in zip(gram, xin)]
    n_levels = c.bit_length() - 1
    for level in range(n_levels):
        nxt = []
        for r_ in rj:
            hi = bf(r_)
            if level < n_levels - NEUMANN_SINGLE_PASS_LEVELS:
                lo = bf(r_ - hi.astype(F32))
                both = _dot(jnp.concatenate([hi[:, 0:n], lo[:, 0:n]], axis=0), hi)
                p = both[0:c] + both[c:] + _dot(hi[:, 0:n], lo)
            else:
                p = _dot(hi[:, 0:n], hi)
            nxt.append(p + jnp.where(right, r_, 0.0))
        rj = nxt
    uv = [jnp.concatenate([bf(r_[:, n:]), vi], axis=0) for r_, vi in zip(rj, v_h)]
    y = [a[c:] + _dot(bf(g[c:]), uvi) for a, g, uvi in zip(ls, gram, uv)]
    s_new = [(si + _dot_tn(uvi, r_)) * pci for si, uvi, r_, pci in zip(s_h, uv, rhs_h, pc_h)]
    per = RWKV_HEADS
    for b in range(nb):
        for d in range(2):
            first = (b * 2 + d) * per
            y_refs[d][b] = jnp.concatenate(y[first:first + per], axis=1)
            s_ref[b, d] = jnp.concatenate(s_new[first:first + per], axis=1)


def _rwkv_scan_call(prep, batch, s_tot, ctx_len):
    v, _, _, lhs0, rhs0, lhs1, rhs1, pc = prep
    n_ch = s_tot // CHUNK
    n_ctx_ch = ctx_len // CHUNK
    per_tile = ROW_TILE // CHUNK
    c = RWKV_DIM
    nb = SCAN_BATCH if batch % SCAN_BATCH == 0 else 1

    def chunk_of(d, i):
        if d == 0:
            return i
        return jnp.where(i < n_ctx_ch, n_ctx_ch - 1 - i, n_ctx_ch + n_ch - 1 - i)

    def stacked_spec(d):
        return pl.BlockSpec((nb, None, 2 * CHUNK, c), lambda b, i: (b, chunk_of(d, i), 0, 0))

    def row_spec(d):
        return pl.BlockSpec((nb, CHUNK, c), lambda b, i: (b, chunk_of(d, i), 0))

    def pc_spec(d):
        def index(b, i):
            ch = chunk_of(d, i)
            return (b, (ch // per_tile) * 2 * per_tile + d * per_tile + ch % per_tile, 0, 0)
        return pl.BlockSpec((nb, None, 1, c), index)

    v3 = v.reshape(batch, s_tot, c)
    pc4 = pc.reshape(batch, pc.shape[0] // batch, 1, c)
    ins, specs = [], []
    for d, (lhs, rhs) in enumerate(((lhs0, rhs0), (lhs1, rhs1))):
        ins += [lhs.reshape(batch, n_ch, 2 * CHUNK, c), rhs.reshape(batch, n_ch, 2 * CHUNK, c), v3, pc4]
        specs += [stacked_spec(d), stacked_spec(d), row_spec(d), pc_spec(d)]
    y0, y1 = pl.pallas_call(
        functools.partial(_rwkv_scan_kernel, nb=nb),
        out_shape=[jax.ShapeDtypeStruct((batch, s_tot, c), F32)] * 2,
        grid=(batch // nb, n_ch),
        in_specs=specs,
        out_specs=[row_spec(0), row_spec(1)],
        scratch_shapes=[pltpu.VMEM((nb, 2, RWKV_HEAD, c), F32)],
        compiler_params=_cparams(2),
        name="rwkv_scan",
    )(*ins)
    return y0.reshape(batch * s_tot, c), y1.reshape(batch * s_tot, c)


def _merge_kernel(x_ref, g1_ref, sh_ref, sc_ref, gt_ref, a_ref, y0_ref, y1_ref, bonus_ref, rg_ref, m_ref,
                  lnw_ref, lnb_ref, bd_ref, wg_ref, wb_ref, wo_ref, o_ref):
    x = x_ref[...]
    h = (_rms(x, g1_ref[...]) * (1.0 + sc_ref[...]) + sh_ref[...]).astype(BF16)

    bd = bd_ref[...]
    y = y0_ref[...] + y1_ref[...]
    mu = _group_sum(y, bd) * (1.0 / RWKV_HEAD)
    yc = y - mu
    var = _group_sum(yc * yc, bd) * (1.0 / RWKV_HEAD)
    yn = yc * lax.rsqrt(var + LNX_EPS) * lnw_ref[...] + lnb_ref[...]
    r_out = ((yn + bonus_ref[...]) * rg_ref[...]).astype(BF16)

    d = D_MODEL
    mixed = None
    for i, yb in enumerate((a_ref[...], r_out, m_ref[...])):
        gate = jax.nn.sigmoid(_dot(h, wg_ref[:, i * d:(i + 1) * d]))
        term = gate * _dot(yb, wb_ref[i])
        mixed = term if mixed is None else mixed + term
    o_ref[...] = x + gt_ref[...] * _dot(mixed.astype(BF16), wo_ref[...])


def _merge_call(xs, mod, g1, a, y0, y1, bonus, rg, m, lnw, lnb, bd, wg, wb, wo, geom, latent_only):
    n_rows, n_tiles, n_ctx_tiles, ctx_row = geom
    c = RWKV_DIM
    if latent_only:
        n_lat = n_tiles - n_ctx_tiles
        n_out_tiles = (n_rows // ROW_TILE) // n_tiles * n_lat
        src_tile = lambda i: (i // n_lat) * n_tiles + n_ctx_tiles + i % n_lat
        ms = lambda which: _row_mod_spec(which, n_lat, 0, ctx_row)
    else:
        n_out_tiles = n_rows // ROW_TILE
        src_tile = lambda i: i
        ms = lambda which: _row_mod_spec(which, n_tiles, n_ctx_tiles, ctx_row)
    rows_in = lambda width: pl.BlockSpec((ROW_TILE, width), lambda i: (src_tile(i), 0))
    return pl.pallas_call(
        _merge_kernel,
        out_shape=jax.ShapeDtypeStruct((n_out_tiles * ROW_TILE, D_MODEL), F32),
        grid=(n_out_tiles,),
        in_specs=[
            rows_in(D_MODEL), _full((1, D_MODEL)), ms(0), ms(1), ms(2),
            rows_in(c), rows_in(c), rows_in(c), rows_in(c), rows_in(c), rows_in(c),
            _full(lnw.shape), _full(lnb.shape), _full(bd.shape),
            _full(wg.shape), _full(wb.shape), _full(wo.shape),
        ],
        out_specs=_rows(D_MODEL),
        compiler_params=_cparams(1),
        name="merge",
    )(xs, g1, mod, mod, mod, a, y0, y1, bonus, rg, m, lnw, lnb, bd, wg, wb, wo)


def _mlp_kernel(x_ref, g2_ref, sh_ref, sc_ref, gt_ref, w1_ref, w2_ref, gf_ref, o_ref, *, final_norm):
    x = x_ref[...]
    h = (_rms(x, g2_ref[...]) * (1.0 + sc_ref[...]) + sh_ref[...]).astype(BF16)
    acc = None
    blk = D_MODEL
    for j in range(D_FF // blk):
        a = jnp.maximum(_dot(h, w1_ref[:, j * blk:(j + 1) * blk]), 0.0)
        part = _dot((a * a).astype(BF16), w2_ref[j * blk:(j + 1) * blk, :])
        acc = part if acc is None else acc + part
    out = x + gt_ref[...] * acc
    if final_norm:
        out = _rms(out, gf_ref[...])
    o_ref[...] = out


def _mlp_call(xs, mod, g2, w1, w2, gf, geom, final_norm):
    n_rows, n_tiles, n_ctx_tiles, ctx_row = geom
    ms = lambda which: _row_mod_spec(which, n_tiles, n_ctx_tiles, ctx_row)
    return pl.pallas_call(
        functools.partial(_mlp_kernel, final_norm=final_norm),
        out_shape=jax.ShapeDtypeStruct((n_rows, D_MODEL), F32),
        grid=(n_rows // ROW_TILE,),
        in_specs=[
            _rows(D_MODEL), _full((1, D_MODEL)), ms(3), ms(4), ms(5),
            _full(w1.shape), _full(w2.shape), _full((1, D_MODEL)),
        ],
        out_specs=_rows(D_MODEL),
        compiler_params=_cparams(1),
        name="mlp",
    )(xs, g2, mod, mod, mod, w1, w2, gf)


def _rope_tables(ctx_len, seq):
    pos = jnp.arange(seq, dtype=jnp.int32)
    rows, cols = pos // GRID_W, pos % GRID_W

    def cs(p, n):
        inv = ROPE_THETA ** (-jnp.arange(n, dtype=F32) / n)
        ang = p.astype(F32)[:, None] * inv[None, :]
        return jnp.cos(ang), jnp.sin(ang)

    def axial(n):
        cr, sr = cs(rows, n)
        cc, sc = cs(cols, n)
        return jnp.concatenate([cr, cr, cc, cc], -1), jnp.concatenate([-sr, sr, -sc, sc], -1)

    def with_ctx(cos, sin):
        w = cos.shape[1]
        return (jnp.concatenate([jnp.ones((ctx_len, w), F32), cos], 0),
                jnp.concatenate([jnp.zeros((ctx_len, w), F32), sin], 0))

    gc, gs = axial(GQA_HEAD_DIM // 4)
    gc, gs = jnp.tile(gc, (1, 2)), jnp.tile(gs, (1, 2))
    mc, ms = axial(QK_ROPE // 4)
    pad = MLA_SLOT - QK_NOPE - QK_ROPE
    mc = jnp.concatenate([jnp.ones((seq, QK_NOPE), F32), mc, jnp.ones((seq, pad), F32)], -1)
    ms = jnp.concatenate([jnp.zeros((seq, QK_NOPE), F32), ms, jnp.zeros((seq, pad), F32)], -1)
    return with_ctx(gc, gs) + with_ctx(mc, ms)


def _block_ones(n, blk):
    i = jnp.arange(n)
    return (i[:, None] // blk == i[None, :] // blk)


def _chunk_tri():
    i = jnp.arange(ROW_TILE)
    same = _block_ones(ROW_TILE, CHUNK)
    fwd = jnp.logical_and(same, i[None, :] <= i[:, None])
    bwd = jnp.logical_and(same, i[None, :] >= i[:, None])
    return jnp.stack([fwd, bwd]).astype(BF16)


def _pack_layer(l, w_in, gqa_q_gain, gqa_k_gain, mla_q_up, mla_kv_up, rwkv_w2, rwkv_a2):
    w = w_in[l]
    o_q = 0
    o_rw = GQA_HEADS * GQA_HEAD_DIM + 2 * GQA_KV_HEADS * GQA_HEAD_DIM
    o_qd = o_rw + RWKV_IN
    o_kr = o_qd + Q_LORA + KV_LORA
    o_gate = o_kr + QK_ROPE
    zeros = lambda n: jnp.zeros((D_MODEL, n), F32)
    w_main = jnp.concatenate(
        [w[:, o_q:o_kr], zeros(QK_NOPE), w[:, o_kr:o_gate], zeros(MLA_SLOT - QK_NOPE - QK_ROPE)], axis=1)
    w_gate = w[:, o_gate:]
    gains = jnp.concatenate([jnp.tile(gqa_q_gain[l], GQA_HEADS), jnp.tile(gqa_k_gain[l], GQA_KV_HEADS)])[None, :]

    qup = mla_q_up[l].reshape(Q_LORA, MLA_HEADS, QK_NOPE + QK_ROPE)
    qup = jnp.pad(qup, ((0, 0), (0, 0), (0, MLA_SLOT - QK_NOPE - QK_ROPE))).reshape(Q_LORA, MLA_HEADS * MLA_SLOT)
    kvu = mla_kv_up[l].reshape(KV_LORA, MLA_HEADS, QK_NOPE + V_HEAD)
    kvk = jnp.pad(kvu[:, :, :QK_NOPE], ((0, 0), (0, 0), (0, MLA_SLOT - QK_NOPE))).reshape(KV_LORA, MLA_HEADS * MLA_SLOT)
    kvv = kvu[:, :, QK_NOPE:].reshape(KV_LORA, MLA_HEADS * V_HEAD)

    def pad_dir(w2):
        lora = w2.shape[1]
        z = jnp.zeros_like(w2[0])
        return jnp.stack([jnp.concatenate([w2[0], z], 0), jnp.concatenate([z, w2[1]], 0)]).astype(BF16)

    return dict(w_main=w_main.astype(BF16), w_gate=w_gate.astype(BF16), gains=gains,
                qup=qup.astype(BF16), kvk=kvk.astype(BF16), kvv=kvv.astype(BF16),
                w2p=pad_dir(rwkv_w2[l]), a2p=pad_dir(rwkv_a2[l]))


def kernel(x, c, ctx, c_ctx, w_mod, b_mod, g_norm1, g_norm2, w_in, gqa_q_gain, gqa_k_gain, rwkv_shift_mu, rwkv_w0, rwkv_w2, rwkv_a0, rwkv_a2, rwkv_g2, rwkv_k_k, rwkv_k_a, rwkv_r_k, rwkv_ln_w, rwkv_ln_b, mla_q_norm, mla_q_up, mla_kv_norm, mla_kv_up, w_branch, w_out, w_ff1, w_ff2, g_final):
    batch, seq, d = x.shape
    ctx_len = ctx.shape[1]
    depth = w_mod.shape[0]
    s_tot = ctx_len + seq
    assert d == D_MODEL and batch < 8
    assert ctx_len % ROW_TILE == 0 and seq % ROW_TILE == 0 and seq % GRID_W == 0
    n_rows = batch * s_tot
    geom = (n_rows, s_tot // ROW_TILE, ctx_len // ROW_TILE, batch)

    c_all = jnp.concatenate([c, c_ctx[None, :], jnp.zeros((8 - batch - 1, d), F32)], axis=0)
    mod_all = _mod_call(c_all, w_mod, b_mod).reshape(depth, 8, 6, 1, d)

    tabs = _rope_tables(ctx_len, seq)
    bd = _block_ones(LANES, RWKV_HEAD).astype(BF16)
    tri = _chunk_tri()
    row1 = lambda v: v.reshape(1, -1)

    xs = jnp.concatenate([ctx, x], axis=1).reshape(n_rows, d)
    for l in range(depth):
        pk = _pack_layer(l, w_in, gqa_q_gain, gqa_k_gain, mla_q_up, mla_kv_up, rwkv_w2, rwkv_a2)
        mod = mod_all[l]
        g1 = row1(g_norm1[l])
        gq, gk, gv, ru, mq, mk, mv = _inproj_call(
            xs, mod, g1, pk["w_main"], pk["gains"], bd, tabs,
            row1(mla_q_norm[l]), pk["qup"], row1(mla_kv_norm[l]), pk["kvk"], pk["kvv"], geom)
        a_out = _attn_call(gq, gk, gv, heads=GQA_HEADS, group=GQA_HEADS // GQA_KV_HEADS, dq=GQA_HEAD_DIM,
                           dv=GQA_HEAD_DIM, batch=batch, s_tot=s_tot, ctx_len=ctx_len, name="gqa_attn")
        m_out = _attn_call(mq, mk, mv, heads=MLA_HEADS, group=1, dq=MLA_SLOT, dv=V_HEAD,
                           batch=batch, s_tot=s_tot, ctx_len=ctx_len, name="mla_attn")
        prep = _rwkv_prep_call(ru, rwkv_shift_mu[l], row1(rwkv_k_k[l]), row1(rwkv_k_a[l]), row1(rwkv_r_k[l]),
                               rwkv_w0[l], pk["w2p"], rwkv_a0[l], pk["a2p"], rwkv_g2[l].astype(BF16), bd, tri, geom)
        y0, y1 = _rwkv_scan_call(prep, batch, s_tot, ctx_len)
        last = l == depth - 1
        xs = _merge_call(xs, mod, g1, a_out, y0, y1, prep[1], prep[2], m_out,
                         row1(rwkv_ln_w[l]), row1(rwkv_ln_b[l]), bd, pk["w_gate"],
                         w_branch[l].astype(BF16), w_out[l].astype(BF16), geom, latent_only=last)
        mlp_geom = (batch * seq, seq // ROW_TILE, 0, batch) if last else geom
        xs = _mlp_call(xs, mod, row1(g_norm2[l]), w_ff1[l].astype(BF16), w_ff2[l].astype(BF16),
                       row1(g_final), mlp_geom, final_norm=last)
    return xs.reshape(batch, seq, d)
```

```python
import functools

import jax
import jax.numpy as jnp
from jax import lax
from jax.experimental import pallas as pl
from jax.experimental.pallas import tpu as pltpu

F32 = jnp.float32
BF16 = jnp.bfloat16

D_MODEL = 1024
GRID_W = 64
ROPE_THETA = 10000.0
NORM_EPS = 1e-6

GQA_HEADS = 8
GQA_KV_HEADS = 2
GQA_HEAD_DIM = 64
GQA_SCALE = GQA_HEAD_DIM ** -0.5

RWKV_HEADS = 8
RWKV_HEAD = 64
RWKV_DIM = RWKV_HEADS * RWKV_HEAD
DECAY_LORA = 64
AAA_LORA = 64
GATE_LORA = 128
LNX_EPS = 64e-5
RWKV_IN = 3 * RWKV_DIM + 2 * DECAY_LORA + 2 * AAA_LORA + GATE_LORA

MLA_HEADS = 8
Q_LORA = 384
KV_LORA = 256
QK_NOPE = 64
QK_ROPE = 32
V_HEAD = 64
MLA_SCALE = (QK_NOPE + QK_ROPE) ** -0.5
MLA_SLOT = 128
ATTN_KEY_BLOCK = 512
LOG2_E = 1.4426950408889634

N_BRANCH = 3
BRANCH_W = 512
D_FF = 4 * D_MODEL

LANES = 128
ROW_TILE = 256
CHUNK = 64
SCAN_BATCH = 4
NEUMANN_SINGLE_PASS_LEVELS = 1
VMEM_LIMIT = 56 * 1024 * 1024

_C_GQ = 0
_C_GK = _C_GQ + GQA_HEADS * GQA_HEAD_DIM
_C_GV = _C_GK + GQA_KV_HEADS * GQA_HEAD_DIM
_C_RW = _C_GV + GQA_KV_HEADS * GQA_HEAD_DIM
_C_QD = _C_RW + RWKV_IN
_C_KVD = _C_QD + Q_LORA
_C_KR = _C_KVD + KV_LORA


def _cparams(n_axes):
    return pltpu.CompilerParams(dimension_semantics=("arbitrary",) * n_axes,
                                vmem_limit_bytes=VMEM_LIMIT)


def _dot(a, b):
    return jnp.dot(a, b, preferred_element_type=F32)


def _dot_nt(a, b):
    return lax.dot_general(a, b, (((1,), (1,)), ((), ())), preferred_element_type=F32)


def _dot_tn(a, b):
    return lax.dot_general(a, b, (((0,), (0,)), ((), ())), preferred_element_type=F32)


def _split(x):
    hi = x.astype(BF16)
    lo = (x - hi.astype(F32)).astype(BF16)
    return hi, lo


def _dot3(a, b, dot=_dot):
    m = a[0].shape[0]
    both = dot(jnp.concatenate([a[0], a[1]], axis=0), b[0])
    return both[0:m] + both[m:] + dot(a[0], b[1])


def _dot3_tn(a, b):
    m = a[0].shape[1]
    both = _dot_tn(jnp.concatenate([a[0], a[1]], axis=1), b[0])
    return both[0:m] + both[m:] + _dot_tn(a[0], b[1])


def _group_sum(x, bd):
    outs = []
    for g in range(x.shape[1] // LANES):
        hi, lo = _split(x[:, g * LANES:(g + 1) * LANES])
        outs.append(_dot(hi, bd) + _dot(lo, bd))
    return outs[0] if len(outs) == 1 else jnp.concatenate(outs, axis=1)


def _rope(x, cos, sin, half):
    lane = lax.broadcasted_iota(jnp.int32, x.shape, 1)
    up = pltpu.roll(x, LANES - half, axis=1)
    dn = pltpu.roll(x, half, axis=1)
    sw = jnp.where((lane % (2 * half)) < half, up, dn)
    return x * cos + sw * sin


def _rms(x, gain):
    ms = jnp.mean(x * x, axis=-1, keepdims=True)
    return x * lax.rsqrt(ms + NORM_EPS) * gain


def _mod_kernel(c_ref, w_ref, b_ref, o_ref):
    c = c_ref[...]
    s = (c * jax.nn.sigmoid(c)).astype(BF16)
    o_ref[...] = _dot(s, w_ref[...].astype(BF16)) + b_ref[...]


def _mod_call(c_all, w_mod, b_mod):
    n_layers, d, n = w_mod.shape
    bn = 1536
    return pl.pallas_call(
        _mod_kernel,
        out_shape=jax.ShapeDtypeStruct((n_layers, 8, n), F32),
        grid=(n_layers, n // bn),
        in_specs=[
            pl.BlockSpec((8, d), lambda l, j: (0, 0)),
            pl.BlockSpec((None, d, bn), lambda l, j: (l, 0, j)),
            pl.BlockSpec((None, 1, bn), lambda l, j: (l, 0, j)),
        ],
        out_specs=pl.BlockSpec((None, 8, bn), lambda l, j: (l, 0, j)),
        compiler_params=_cparams(2),
        name="adaln_mod",
    )(c_all, w_mod, b_mod.reshape(n_layers, 1, n))


def _inproj_kernel(x_ref, g1_ref, sh_ref, sc_ref, w_ref, wkr_ref, gain_ref, bd_ref,
                   gc_ref, gs_ref, mc_ref, ms_ref,
                   qn_ref, qup_ref, kvn_ref, kvk_ref, kvv_ref,
                   gq_ref, gk_ref, gv_ref, ru_ref, mq_ref, mk_ref, mv_ref):
    x = x_ref[...]
    h = (_rms(x, g1_ref[...]) * (1.0 + sc_ref[...]) + sh_ref[...]).astype(BF16)
    p = _dot(h, w_ref[...])

    bd = bd_ref[...]
    gc, gs = gc_ref[...], gs_ref[...]
    n_q_slabs = GQA_HEADS * GQA_HEAD_DIM // LANES
    n_k_slabs = GQA_KV_HEADS * GQA_HEAD_DIM // LANES
    for g in range(n_q_slabs + n_k_slabs):
        slab = p[:, g * LANES:(g + 1) * LANES]
        ms = _group_sum(slab * slab, bd) * (1.0 / GQA_HEAD_DIM)
        y = slab * lax.rsqrt(ms + NORM_EPS) * gain_ref[:, g * LANES:(g + 1) * LANES]
        y = _rope(y, gc, gs, GQA_HEAD_DIM // 4)
        if g < n_q_slabs:
            gq_ref[:, g * LANES:(g + 1) * LANES] = (y * (GQA_SCALE * LOG2_E)).astype(BF16)
        else:
            gk_ref[:, (g - n_q_slabs) * LANES:(g - n_q_slabs + 1) * LANES] = y.astype(BF16)
    gv_ref[...] = p[:, _C_GV:_C_RW].astype(BF16).T

    ru_ref[...] = p[:, _C_RW:_C_QD]

    mc, msn = mc_ref[...], ms_ref[...]
    qd = _rms(p[:, _C_QD:_C_KVD], qn_ref[...])
    q = _dot(qd.astype(BF16), qup_ref[...])
    kvd = _rms(p[:, _C_KVD:_C_KR], kvn_ref[...]).astype(BF16)
    kn = _dot(kvd, kvk_ref[...])
    kr = _rope(_dot(h, wkr_ref[...]), mc, msn, QK_ROPE // 4)
    for hd in range(MLA_HEADS):
        sl = slice(hd * MLA_SLOT, (hd + 1) * MLA_SLOT)
        mq_ref[:, sl] = (_rope(q[:, sl], mc, msn, QK_ROPE // 4) * (MLA_SCALE * LOG2_E)).astype(BF16)
        mk_ref[:, sl] = (kn[:, sl] + kr).astype(BF16)
    mv_ref[...] = _dot(kvd, kvv_ref[...]).astype(BF16).T


def _row_mod_spec(which, n_tiles, n_ctx_tiles, ctx_row):
    def index(i):
        b, t = i // n_tiles, i % n_tiles
        return (jnp.where(t < n_ctx_tiles, ctx_row, b), which, 0, 0)
    return pl.BlockSpec((None, None, 1, D_MODEL), index)


def _full(shape):
    return pl.BlockSpec(shape, lambda i: (0,) * len(shape))


def _rows(width):
    return pl.BlockSpec((ROW_TILE, width), lambda i: (i, 0))


def _inproj_call(xs, mod, g1, w_main, w_kr, gains, bd, tabs, qn, qup, kvn, kvk, kvv, geom):
    n_rows, n_tiles, n_ctx_tiles, ctx_row = geom
    tab_spec = pl.BlockSpec((ROW_TILE, LANES), lambda i: (i % n_tiles, 0))
    widths = (GQA_HEADS * GQA_HEAD_DIM, GQA_KV_HEADS * GQA_HEAD_DIM, GQA_KV_HEADS * GQA_HEAD_DIM,
              RWKV_IN, MLA_HEADS * MLA_SLOT, MLA_HEADS * MLA_SLOT, MLA_HEADS * V_HEAD)
    dtypes = (BF16, BF16, BF16, F32, BF16, BF16, BF16)
    transposed = (False, False, True, False, False, False, True)
    n_batch = n_rows // (n_tiles * ROW_TILE)
    out_shapes = [jax.ShapeDtypeStruct((n_batch, w, n_tiles * ROW_TILE) if tr else (n_rows, w), dt)
                  for w, dt, tr in zip(widths, dtypes, transposed)]
    out_specs = [pl.BlockSpec((None, w, ROW_TILE), lambda i: (i // n_tiles, 0, i % n_tiles)) if tr else _rows(w)
                 for w, tr in zip(widths, transposed)]
    return pl.pallas_call(
        _inproj_kernel,
        out_shape=out_shapes,
        grid=(n_rows // ROW_TILE,),
        in_specs=[
            _rows(D_MODEL), _full((1, D_MODEL)),
            _row_mod_spec(0, n_tiles, n_ctx_tiles, ctx_row),
            _row_mod_spec(1, n_tiles, n_ctx_tiles, ctx_row),
            _full(w_main.shape), _full(w_kr.shape), _full(gains.shape), _full(bd.shape),
            tab_spec, tab_spec, tab_spec, tab_spec,
            _full(qn.shape), _full(qup.shape), _full(kvn.shape), _full(kvk.shape), _full(kvv.shape),
        ],
        out_specs=out_specs,
        compiler_params=_cparams(1),
        name="in_proj",
    )(xs, g1, mod, mod, w_main, w_kr, gains, bd, *tabs, qn, qup, kvn, kvk, kvv)


def _attn_kernel(q_ref, k_ref, vt_ref, o_ref, s_ref, *, heads, group, dq, dv, n_ctx_tiles, ctx_len, s_tot):
    t = pl.program_id(1)
    sub = 8

    def run(nk):
        blocks = [(lo, min(lo + ATTN_KEY_BLOCK, nk)) for lo in range(0, nk, ATTN_KEY_BLOCK)]

        def score_block(hq, lo, hi, m8):
            hk = hq // group
            sj = _dot_nt(k_ref[lo:hi, hk * dq:(hk + 1) * dq], q_ref[:, hq * dq:(hq + 1) * dq])
            s_ref[hq % 2, lo:hi, :] = sj
            mj = jnp.max(sj.reshape((hi - lo) // sub, sub, ROW_TILE), axis=0)
            return mj if m8 is None else jnp.maximum(m8, mj)

        def prob_block(hq, lo, hi, m, l8, acc):
            hk = hq // group
            e = jnp.exp2(s_ref[hq % 2, lo:hi, :] - m)
            lj = jnp.sum(e.reshape((hi - lo) // sub, sub, ROW_TILE), axis=0)
            pj = _dot(vt_ref[hk * dv:(hk + 1) * dv, lo:hi], e.astype(BF16))
            return (lj if l8 is None else l8 + lj), (pj if acc is None else acc + pj)

        m8 = None
        for lo, hi in blocks:
            m8 = score_block(0, lo, hi, m8)
        done = []
        for hq in range(heads):
            m = jnp.max(m8, axis=0, keepdims=True)
            m8, l8, acc = None, None, None
            for lo, hi in blocks:
                if hq + 1 < heads:
                    m8 = score_block(hq + 1, lo, hi, m8)
                l8, acc = prob_block(hq, lo, hi, m, l8, acc)
            l = jnp.sum(l8, axis=0, keepdims=True)
            done.append(acc / l)
            if len(done) * dv == LANES:
                c0 = (hq + 1) * dv - LANES
                o_ref[:, c0:c0 + LANES] = jnp.concatenate(done, axis=0).T.astype(o_ref.dtype)
                done.clear()

    @pl.when(t < n_ctx_tiles)
    def _():
        run(ctx_len)

    @pl.when(t >= n_ctx_tiles)
    def _():
        run(s_tot)


def _attn_call(q, k, vt, *, heads, group, dq, dv, batch, s_tot, ctx_len, name):
    n_tiles = s_tot // ROW_TILE
    assert ctx_len % LANES == 0 and s_tot % LANES == 0
    kern = functools.partial(_attn_kernel, heads=heads, group=group, dq=dq, dv=dv,
                             n_ctx_tiles=ctx_len // ROW_TILE, ctx_len=ctx_len, s_tot=s_tot)
    return pl.pallas_call(
        kern,
        out_shape=jax.ShapeDtypeStruct((batch * s_tot, heads * dv), BF16),
        grid=(batch, n_tiles),
        in_specs=[
            pl.BlockSpec((ROW_TILE, q.shape[1]), lambda b, t: (b * n_tiles + t, 0)),
            pl.BlockSpec((s_tot, k.shape[1]), lambda b, t: (b, 0)),
            pl.BlockSpec((None, vt.shape[1], s_tot), lambda b, t: (b, 0, 0)),
        ],
        out_specs=pl.BlockSpec((ROW_TILE, heads * dv), lambda b, t: (b * n_tiles + t, 0)),
        scratch_shapes=[pltpu.VMEM((2, s_tot, ROW_TILE), F32)],
        compiler_params=_cparams(2),
        name=name,
    )(q, k, vt)


def _rwkv_prep_kernel(u_ref, prev_ref, next_ref, mu_ref, kk_ref, ka_ref, rk_ref, w0_ref, w2_ref,
                      a0_ref, a2_ref, g2_ref, bd_ref, tri_ref,
                      v_ref, bonus_ref, gate_ref, lhs0_ref, rhs0_ref, lhs1_ref, rhs1_ref, pc_ref,
                      *, n_tiles, n_ctx_tiles):
    t = pl.program_id(0) % n_tiles
    u = u_ref[...]
    rows = u.shape[0]
    sub = 8
    row = lax.broadcasted_iota(jnp.int32, (sub, u.shape[1]), 0)
    seq_start = jnp.logical_or(t == 0, t == n_ctx_tiles)
    seq_end = jnp.logical_or(t == n_ctx_tiles - 1, t == n_tiles - 1)
    halo_prev = prev_ref[sub - 1:sub, :] * jnp.where(seq_start, 0.0, 1.0)
    halo_next = next_ref[0:1, :] * jnp.where(seq_end, 0.0, 1.0)
    prev = pltpu.roll(u, 1, axis=0)
    nxt = pltpu.roll(u, rows - 1, axis=0)
    prev = jnp.concatenate([jnp.where(row == 0, halo_prev, prev[0:sub]), prev[sub:]], axis=0)
    nxt = jnp.concatenate([nxt[0:rows - sub], jnp.where(row == sub - 1, halo_next, nxt[rows - sub:])], axis=0)
    mu_p, mu_n = mu_ref[0:1, :], mu_ref[1:2, :]
    us = u * (1.0 - mu_p - mu_n) + mu_p * prev + mu_n * nxt

    c = RWKV_DIM
    r, k, v = us[:, 0:c], us[:, c:2 * c], us[:, 2 * c:3 * c]
    wd = jnp.tanh(us[:, 3 * c:3 * c + LANES]).astype(BF16)
    ad = us[:, 3 * c + LANES:3 * c + 2 * LANES].astype(BF16)
    gd = us[:, 3 * c + 2 * LANES:3 * c + 3 * LANES]
    bd = bd_ref[...]

    kk = k * kk_ref[...]
    kk = kk / jnp.maximum(jnp.sqrt(_group_sum(kk * kk, bd)), 1e-12)

    v_ref[...] = v.astype(BF16)
    gate_ref[...] = _dot(jax.nn.sigmoid(gd).astype(BF16), g2_ref[...])

    outs = ((lhs0_ref, rhs0_ref), (lhs1_ref, rhs1_ref))
    n_chunks = rows // CHUNK
    bonus = None
    for d in range(2):
        z = w0_ref[d:d + 1, :] + _dot(wd, w2_ref[d])
        nz = -z
        softplus = jnp.maximum(nz, 0.0) + jnp.log(1.0 + jnp.exp(-jnp.abs(nz)))
        lw = -jnp.exp(-softplus - 0.5)
        a = jax.nn.sigmoid(a0_ref[d:d + 1, :] + _dot(ad, a2_ref[d]))
        k_d = k * (1.0 + (a - 1.0) * ka_ref[...])
        b_d = _group_sum(r * k_d * rk_ref[...], bd) * v
        bonus = b_d if bonus is None else bonus + b_d

        tri = tri_ref[d]
        hi = lw.astype(BF16)
        rem = lw - hi.astype(F32)
        mid = rem.astype(BF16)
        lo = (rem - mid.astype(F32)).astype(BF16)
        cum = _dot(tri, hi) + (_dot(tri, mid) + _dot(tri, lo))
        e_neg = jnp.exp(-cum)
        lhs_ref, rhs_ref = outs[d]
        rt = (r * jnp.exp(cum)).astype(BF16)
        at = (-kk * jnp.exp(cum - lw)).astype(BF16)
        kt = (k_d * e_neg).astype(BF16)
        bt = (kk * a * e_neg).astype(BF16)
        for ci in range(n_chunks):
            rs = slice(ci * CHUNK, (ci + 1) * CHUNK)
            lhs_ref[ci, 0:CHUNK, :] = at[rs]
            lhs_ref[ci, CHUNK:, :] = rt[rs]
            rhs_ref[ci, 0:CHUNK, :] = bt[rs]
            rhs_ref[ci, CHUNK:, :] = kt[rs]
        last = CHUNK - 1 if d == 0 else 0
        tot = jnp.concatenate([cum[ci * CHUNK + last:ci * CHUNK + last + 1, :] for ci in range(n_chunks)], axis=0)
        pc_ref[d * n_chunks:(d + 1) * n_chunks, :] = jnp.exp(tot)
    bonus_ref[...] = bonus


def _rwkv_prep_call(ru, mu, kk, ka, rk, w0, w2p, a0, a2p, g2, bd, tri, geom):
    n_rows, n_tiles, n_ctx_tiles, _ = geom
    grid = n_rows // ROW_TILE
    g8 = ROW_TILE // 8
    last8 = n_rows // 8 - 1
    kern = functools.partial(_rwkv_prep_kernel, n_tiles=n_tiles, n_ctx_tiles=n_ctx_tiles)
    c = RWKV_DIM
    n_pc = 2 * (ROW_TILE // CHUNK)
    return pl.pallas_call(
        kern,
        out_shape=[jax.ShapeDtypeStruct((n_rows, c), BF16)] + [jax.ShapeDtypeStruct((n_rows, c), F32)] * 2
                  + [jax.ShapeDtypeStruct((n_rows // CHUNK, 2 * CHUNK, c), BF16)] * 4
                  + [jax.ShapeDtypeStruct((grid * n_pc, c), F32)],
        grid=(grid,),
        in_specs=[
            _rows(RWKV_IN),
            pl.BlockSpec((8, RWKV_IN), lambda i: (jnp.maximum(i * g8 - 1, 0), 0)),
            pl.BlockSpec((8, RWKV_IN), lambda i: (jnp.minimum((i + 1) * g8, last8), 0)),
            _full(mu.shape), _full(kk.shape), _full(ka.shape), _full(rk.shape),
            _full(w0.shape), _full(w2p.shape), _full(a0.shape), _full(a2p.shape),
            _full(g2.shape), _full(bd.shape), _full(tri.shape),
        ],
        out_specs=[_rows(c)] * 3
                  + [pl.BlockSpec((ROW_TILE // CHUNK, 2 * CHUNK, c), lambda i: (i, 0, 0))] * 4
                  + [pl.BlockSpec((n_pc, c), lambda i: (i, 0))],
        compiler_params=_cparams(1),
        name="rwkv_prep",
    )(ru, ru, ru, mu, kk, ka, rk, w0, w2p, a0, a2p, g2, bd, tri)


def _rwkv_scan_kernel(*refs, nb):
    ins, (y0_ref, y1_ref, s_ref) = refs[:8], refs[8:]
    y_refs = (y0_ref, y1_ref)

    @pl.when(pl.program_id(1) == 0)
    def _():
        s_ref[...] = jnp.zeros_like(s_ref)

    n = RWKV_HEAD
    c = CHUNK
    ri = lax.broadcasted_iota(jnp.int32, (2 * c, 2 * c), 0)
    ci = lax.broadcasted_iota(jnp.int32, (2 * c, 2 * c), 1)
    tq, ts = ri % c, ci % c
    masks = []
    for d in range(2):
        earlier = (ts < tq) if d == 0 else (ts > tq)
        masks.append(jnp.where(jnp.logical_or(earlier, jnp.logical_and(ri >= c, ts == tq)), 1.0, 0.0))

    chains = [(b, d, hd) for b in range(nb) for d in range(2) for hd in range(RWKV_HEADS)]
    bf = lambda t: t.astype(BF16)

    def head(arr, hd):
        return arr[:, hd * n:(hd + 1) * n]

    lhs_h = [head(ins[4 * d][b], hd) for b, d, hd in chains]
    rhs_h = [head(ins[4 * d + 1][b], hd) for b, d, hd in chains]
    v_h = [head(ins[4 * d + 2][b], hd) for b, d, hd in chains]
    pc_h = [head(ins[4 * d + 3][b], hd) for b, d, hd in chains]
    s_h = [head(s_ref[b, d], hd) for b, d, hd in chains]
    gram = [_dot_nt(l_, r_) * masks[d] for (_, d, _), l_, r_ in zip(chains, lhs_h, rhs_h)]
    ls = [_dot_nt(l_, bf(si)) for l_, si in zip(lhs_h, s_h)]
    zeros = jnp.zeros((c, n), BF16)
    xin = [a[0:c] + _dot(bf(g[0:c]), jnp.concatenate([zeros, vi], axis=0))
           for a, g, vi in zip(ls, gram, v_h)]
    right = lax.broadcasted_iota(jnp.int32, (c, 2 * n), 1) >= n
    rj = [jnp.concatenate([g[0:c, 0:n], xi], axis=1) for g, xi in zip(gram, xin)]
    n_levels = c.bit_length() - 1
    for level in range(n_levels):
        nxt = []
        for r_ in rj:
            hi = bf(r_)
            if level < n_levels - NEUMANN_SINGLE_PASS_LEVELS:
                lo = bf(r_ - hi.astype(F32))
                both = _dot(jnp.concatenate([hi[:, 0:n], lo[:, 0:n]], axis=0), hi)
                p = both[0:c] + both[c:] + _dot(hi[:, 0:n], lo)
            else:
                p = _dot(hi[:, 0:n], hi)
            nxt.append(p + jnp.where(right, r_, 0.0))
        rj = nxt
    uv = [jnp.concatenate([bf(r_[:, n:]), vi], axis=0) for r_, vi in zip(rj, v_h)]
    y = [a[c:] + _dot(bf(g[c:]), uvi) for a, g, uvi in zip(ls, gram, uv)]
    s_new = [(si + _dot_tn(uvi, r_)) * pci for si, uvi, r_, pci in zip(s_h, uv, rhs_h, pc_h)]
    per = RWKV_HEADS
    for b in range(nb):
        for d in range(2):
            first = (b * 2 + d) * per
            y_refs[d][b] = jnp.concatenate(y[first:first + per], axis=1)
            s_ref[b, d] = jnp.concatenate(s_new[first:first + per], axis=1)


def _rwkv_scan_call(prep, batch, s_tot, ctx_len):
    v, _, _, lhs0, rhs0, lhs1, rhs1, pc = prep
    n_ch = s_tot // CHUNK
    n_ctx_ch = ctx_len // CHUNK
    per_tile = ROW_TILE // CHUNK
    c = RWKV_DIM
    nb = SCAN_BATCH if batch % SCAN_BATCH == 0 else 1

    def chunk_of(d, i):
        if d == 0:
            return i
        return jnp.where(i < n_ctx_ch, n_ctx_ch - 1 - i, n_ctx_ch + n_ch - 1 - i)

    def stacked_spec(d):
        return pl.BlockSpec((nb, None, 2 * CHUNK, c), lambda b, i: (b, chunk_of(d, i), 0, 0))

    def row_spec(d):
        return pl.BlockSpec((nb, CHUNK, c), lambda b, i: (b, chunk_of(d, i), 0))

    def pc_spec(d):
        def index(b, i):
            ch = chunk_of(d, i)
            return (b, (ch // per_tile) * 2 * per_tile + d * per_tile + ch % per_tile, 0, 0)
        return pl.BlockSpec((nb, None, 1, c), index)

    v3 = v.reshape(batch, s_tot, c)
    pc4 = pc.reshape(batch, pc.shape[0] // batch, 1, c)
    ins, specs = [], []
    for d, (lhs, rhs) in enumerate(((lhs0, rhs0), (lhs1, rhs1))):
        ins += [lhs.reshape(batch, n_ch, 2 * CHUNK, c), rhs.reshape(batch, n_ch, 2 * CHUNK, c), v3, pc4]
        specs += [stacked_spec(d), stacked_spec(d), row_spec(d), pc_spec(d)]
    y0, y1 = pl.pallas_call(
        functools.partial(_rwkv_scan_kernel, nb=nb),
        out_shape=[jax.ShapeDtypeStruct((batch, s_tot, c), F32)] * 2,
        grid=(batch // nb, n_ch),
        in_specs=specs,
        out_specs=[row_spec(0), row_spec(1)],
        scratch_shapes=[pltpu.VMEM((nb, 2, RWKV_HEAD, c), F32)],
        compiler_params=_cparams(2),
        name="rwkv_scan",
    )(*ins)
    return y0.reshape(batch * s_tot, c), y1.reshape(batch * s_tot, c)


def _merge_kernel(x_ref, g1_ref, sh_ref, sc_ref, gt_ref, a_ref, y0_ref, y1_ref, bonus_ref, rg_ref, m_ref,
                  lnw_ref, lnb_ref, bd_ref, wg_ref, wb_ref, wo_ref, o_ref):
    x = x_ref[...]
    h = (_rms(x, g1_ref[...]) * (1.0 + sc_ref[...]) + sh_ref[...]).astype(BF16)

    bd = bd_ref[...]
    y = y0_ref[...] + y1_ref[...]
    mu = _group_sum(y, bd) * (1.0 / RWKV_HEAD)
    yc = y - mu
    var = _group_sum(yc * yc, bd) * (1.0 / RWKV_HEAD)
    yn = yc * lax.rsqrt(var + LNX_EPS) * lnw_ref[...] + lnb_ref[...]
    r_out = ((yn + bonus_ref[...]) * rg_ref[...]).astype(BF16)

    d = D_MODEL
    mixed = None
    for i, yb in enumerate((a_ref[...], r_out, m_ref[...])):
        gate = jax.nn.sigmoid(_dot(h, wg_ref[:, i * d:(i + 1) * d]))
        term = gate * _dot(yb, wb_ref[i])
        mixed = term if mixed is None else mixed + term
    o_ref[...] = x + gt_ref[...] * _dot(mixed.astype(BF16), wo_ref[...])


def _merge_call(xs, mod, g1, a, y0, y1, bonus, rg, m, lnw, lnb, bd, wg, wb, wo, geom, latent_only):
    n_rows, n_tiles, n_ctx_tiles, ctx_row = geom
    c = RWKV_DIM
    if latent_only:
        n_lat = n_tiles - n_ctx_tiles
        n_out_tiles = (n_rows // ROW_TILE) // n_tiles * n_lat
        src_tile = lambda i: (i // n_lat) * n_tiles + n_ctx_tiles + i % n_lat
        ms = lambda which: _row_mod_spec(which, n_lat, 0, ctx_row)
    else:
        n_out_tiles = n_rows // ROW_TILE
        src_tile = lambda i: i
        ms = lambda which: _row_mod_spec(which, n_tiles, n_ctx_tiles, ctx_row)
    rows_in = lambda width: pl.BlockSpec((ROW_TILE, width), lambda i: (src_tile(i), 0))
    return pl.pallas_call(
        _merge_kernel,
        out_shape=jax.ShapeDtypeStruct((n_out_tiles * ROW_TILE, D_MODEL), F32),
        grid=(n_out_tiles,),
        in_specs=[
            rows_in(D_MODEL), _full((1, D_MODEL)), ms(0), ms(1), ms(2),
            rows_in(c), rows_in(c), rows_in(c), rows_in(c), rows_in(c), rows_in(c),
            _full(lnw.shape), _full(lnb.shape), _full(bd.shape),
            _full(wg.shape), _full(wb.shape), _full(wo.shape),
        ],
        out_specs=_rows(D_MODEL),
        compiler_params=_cparams(1),
        name="merge",
    )(xs, g1, mod, mod, mod, a, y0, y1, bonus, rg, m, lnw, lnb, bd, wg, wb, wo)


def _mlp_kernel(x_ref, g2_ref, sh_ref, sc_ref, gt_ref, w1_ref, w2_ref, gf_ref, o_ref, *, final_norm):
    x = x_ref[...]
    h = (_rms(x, g2_ref[...]) * (1.0 + sc_ref[...]) + sh_ref[...]).astype(BF16)
    acc = None
    blk = D_MODEL
    for j in range(D_FF // blk):
        a = jnp.maximum(_dot(h, w1_ref[:, j * blk:(j + 1) * blk]), 0.0)
        part = _dot((a * a).astype(BF16), w2_ref[j * blk:(j + 1) * blk, :])
        acc = part if acc is None else acc + part
    out = x + gt_ref[...] * acc
    if final_norm:
        out = _rms(out, gf_ref[...])
    o_ref[...] = out


def _mlp_call(xs, mod, g2, w1, w2, gf, geom, final_norm):
    n_rows, n_tiles, n_ctx_tiles, ctx_row = geom
    ms = lambda which: _row_mod_spec(which, n_tiles, n_ctx_tiles, ctx_row)
    return pl.pallas_call(
        functools.partial(_mlp_kernel, final_norm=final_norm),
        out_shape=jax.ShapeDtypeStruct((n_rows, D_MODEL), F32),
        grid=(n_rows // ROW_TILE,),
        in_specs=[
            _rows(D_MODEL), _full((1, D_MODEL)), ms(3), ms(4), ms(5),
            _full(w1.shape), _full(w2.shape), _full((1, D_MODEL)),
        ],
        out_specs=_rows(D_MODEL),
        compiler_params=_cparams(1),
        name="mlp",
    )(xs, g2, mod, mod, mod, w1, w2, gf)


def _rope_tables(ctx_len, seq):
    pos = jnp.arange(seq, dtype=jnp.int32)
    rows, cols = pos // GRID_W, pos % GRID_W

    def cs(p, n):
        inv = ROPE_THETA ** (-jnp.arange(n, dtype=F32) / n)
        ang = p.astype(F32)[:, None] * inv[None, :]
        return jnp.cos(ang), jnp.sin(ang)

    def axial(n):
        cr, sr = cs(rows, n)
        cc, sc = cs(cols, n)
        return jnp.concatenate([cr, cr, cc, cc], -1), jnp.concatenate([-sr, sr, -sc, sc], -1)

    def with_ctx(cos, sin):
        w = cos.shape[1]
        return (jnp.concatenate([jnp.ones((ctx_len, w), F32), cos], 0),
                jnp.concatenate([jnp.zeros((ctx_len, w), F32), sin], 0))

    gc, gs = axial(GQA_HEAD_DIM // 4)
    gc, gs = jnp.tile(gc, (1, 2)), jnp.tile(gs, (1, 2))
    mc, ms = axial(QK_ROPE // 4)
    pad = MLA_SLOT - QK_NOPE - QK_ROPE
    mc = jnp.concatenate([jnp.ones((seq, QK_NOPE), F32), mc, jnp.ones((seq, pad), F32)], -1)
    ms = jnp.concatenate([jnp.zeros((seq, QK_NOPE), F32), ms, jnp.zeros((seq, pad), F32)], -1)
    return with_ctx(gc, gs) + with_ctx(mc, ms)


def _block_ones(n, blk):
    i = jnp.arange(n)
    return (i[:, None] // blk == i[None, :] // blk)


def _chunk_tri():
    i = jnp.arange(ROW_TILE)
    same = _block_ones(ROW_TILE, CHUNK)
    fwd = jnp.logical_and(same, i[None, :] <= i[:, None])
    bwd = jnp.logical_and(same, i[None, :] >= i[:, None])
    return jnp.stack([fwd, bwd]).astype(BF16)


def _pack_layer(l, w_in, gqa_q_gain, gqa_k_gain, mla_q_up, mla_kv_up, rwkv_w2, rwkv_a2):
    w = w_in[l]
    o_kr = _C_KR
    o_gate = o_kr + QK_ROPE
    w_main = w[:, 0:o_kr]
    w_kr = jnp.pad(w[:, o_kr:o_gate], ((0, 0), (QK_NOPE, MLA_SLOT - QK_NOPE - QK_ROPE)))
    w_gate = w[:, o_gate:]
    gains = jnp.concatenate([jnp.tile(gqa_q_gain[l], GQA_HEADS), jnp.tile(gqa_k_gain[l], GQA_KV_HEADS)])[None, :]

    qup = mla_q_up[l].reshape(Q_LORA, MLA_HEADS, QK_NOPE + QK_ROPE)
    qup = jnp.pad(qup, ((0, 0), (0, 0), (0, MLA_SLOT - QK_NOPE - QK_ROPE))).reshape(Q_LORA, MLA_HEADS * MLA_SLOT)
    kvu = mla_kv_up[l].reshape(KV_LORA, MLA_HEADS, QK_NOPE + V_HEAD)
    kvk = jnp.pad(kvu[:, :, :QK_NOPE], ((0, 0), (0, 0), (0, MLA_SLOT - QK_NOPE))).reshape(KV_LORA, MLA_HEADS * MLA_SLOT)
    kvv = kvu[:, :, QK_NOPE:].reshape(KV_LORA, MLA_HEADS * V_HEAD)

    def pad_dir(w2):
        lora = w2.shape[1]
        z = jnp.zeros_like(w2[0])
        return jnp.stack([jnp.concatenate([w2[0], z], 0), jnp.concatenate([z, w2[1]], 0)]).astype(BF16)

    return dict(w_main=w_main, w_kr=w_kr, w_gate=w_gate, gains=gains, qup=qup, kvk=kvk, kvv=kvv,
                w2p=pad_dir(rwkv_w2[l]), a2p=pad_dir(rwkv_a2[l]))


def kernel(x, c, ctx, c_ctx, w_mod, b_mod, g_norm1, g_norm2, w_in, gqa_q_gain, gqa_k_gain, rwkv_shift_mu, rwkv_w0, rwkv_w2, rwkv_a0, rwkv_a2, rwkv_g2, rwkv_k_k, rwkv_k_a, rwkv_r_k, rwkv_ln_w, rwkv_ln_b, mla_q_norm, mla_q_up, mla_kv_norm, mla_kv_up, w_branch, w_out, w_ff1, w_ff2, g_final):
    batch, seq, d = x.shape
    ctx_len = ctx.shape[1]
    depth = w_mod.shape[0]
    s_tot = ctx_len + seq
    assert d == D_MODEL and batch < 8
    assert ctx_len % ROW_TILE == 0 and seq % ROW_TILE == 0 and seq % GRID_W == 0
    n_rows = batch * s_tot
    geom = (n_rows, s_tot // ROW_TILE, ctx_len // ROW_TILE, batch)

    c_all = jnp.concatenate([c, c_ctx[None, :], jnp.zeros((8 - batch - 1, d), F32)], axis=0)
    mod_all = _mod_call(c_all, w_mod, b_mod).reshape(depth, 8, 6, 1, d)

    tabs = _rope_tables(ctx_len, seq)
    bd = _block_ones(LANES, RWKV_HEAD).astype(BF16)
    tri = _chunk_tri()
    row1 = lambda v: v.reshape(1, -1)

    w_in, mla_q_up, mla_kv_up, rwkv_g2, w_branch, w_out, w_ff1, w_ff2 = (
        w.astype(BF16) for w in (w_in, mla_q_up, mla_kv_up, rwkv_g2, w_branch, w_out, w_ff1, w_ff2))

    xs = jnp.concatenate([ctx, x], axis=1).reshape(n_rows, d)
    for l in range(depth):
        pk = _pack_layer(l, w_in, gqa_q_gain, gqa_k_gain, mla_q_up, mla_kv_up, rwkv_w2, rwkv_a2)
        mod = mod_all[l]
        g1 = row1(g_norm1[l])
        gq, gk, gv, ru, mq, mk, mv = _inproj_call(
            xs, mod, g1, pk["w_main"], pk["w_kr"], pk["gains"], bd, tabs,
            row1(mla_q_norm[l]), pk["qup"], row1(mla_kv_norm[l]), pk["kvk"], pk["kvv"], geom)
        a_out = _attn_call(gq, gk, gv, heads=GQA_HEADS, group=GQA_HEADS // GQA_KV_HEADS, dq=GQA_HEAD_DIM,
                           dv=GQA_HEAD_DIM, batch=batch, s_tot=s_tot, ctx_len=ctx_len, name="gqa_attn")
        m_out = _attn_call(mq, mk, mv, heads=MLA_HEADS, group=1, dq=MLA_SLOT, dv=V_HEAD,
                           batch=batch, s_tot=s_tot, ctx_len=ctx_len, name="mla_attn")
        prep = _rwkv_prep_call(ru, rwkv_shift_mu[l], row1(rwkv_k_k[l]), row1(rwkv_k_a[l]), row1(rwkv_r_k[l]),
                               rwkv_w0[l], pk["w2p"], rwkv_a0[l], pk["a2p"], rwkv_g2[l], bd, tri, geom)
        y0, y1 = _rwkv_scan_call(prep, batch, s_tot, ctx_len)
        last = l == depth - 1
        xs = _merge_call(xs, mod, g1, a_out, y0, y1, prep[1], prep[2], m_out,
                         row1(rwkv_ln_w[l]), row1(rwkv_ln_b[l]), bd, pk["w_gate"],
                         w_branch[l], w_out[l], geom, latent_only=last)
        mlp_geom = (batch * seq, seq // ROW_TILE, 0, batch) if last else geom
        xs = _mlp_call(xs, mod, row1(g_norm2[l]), w_ff1[l], w_ff2[l],
                       row1(g_final), mlp_geom, final_norm=last)
    return xs.reshape(batch, seq, d)
```

```python
import functools

import jax
import jax.numpy as jnp
from jax import lax
from jax.experimental import pallas as pl
from jax.experimental.pallas import tpu as pltpu

F32 = jnp.float32
BF16 = jnp.bfloat16

D_MODEL = 1024
GRID_W = 64
ROPE_THETA = 10000.0
NORM_EPS = 1e-6

GQA_HEADS = 8
GQA_KV_HEADS = 2
GQA_HEAD_DIM = 64
GQA_SCALE = GQA_HEAD_DIM ** -0.5

RWKV_HEADS = 8
RWKV_HEAD = 64
RWKV_DIM = RWKV_HEADS * RWKV_HEAD
DECAY_LORA = 64
AAA_LORA = 64
GATE_LORA = 128
LNX_EPS = 64e-5
RWKV_IN = 3 * RWKV_DIM + 2 * DECAY_LORA + 2 * AAA_LORA + GATE_LORA

MLA_HEADS = 8
Q_LORA = 384
KV_LORA = 256
QK_NOPE = 64
QK_ROPE = 32
V_HEAD = 64
MLA_SCALE = (QK_NOPE + QK_ROPE) ** -0.5
MLA_SLOT = 128
ATTN_KEY_BLOCK = 512
LOG2_E = 1.4426950408889634

N_BRANCH = 3
BRANCH_W = 512
D_FF = 4 * D_MODEL

LANES = 128
ROW_TILE = 256
CHUNK = 64
SCAN_BATCH = 4
NEUMANN_SINGLE_PASS_LEVELS = 1
VMEM_LIMIT = 56 * 1024 * 1024

_C_GQ = 0
_C_GK = _C_GQ + GQA_HEADS * GQA_HEAD_DIM
_C_GV = _C_GK + GQA_KV_HEADS * GQA_HEAD_DIM
_C_RW = _C_GV + GQA_KV_HEADS * GQA_HEAD_DIM
_C_QD = _C_RW + RWKV_IN
_C_KVD = _C_QD + Q_LORA
_C_KR = _C_KVD + KV_LORA


def _cparams(n_axes):
    return pltpu.CompilerParams(dimension_semantics=("arbitrary",) * n_axes,
                                vmem_limit_bytes=VMEM_LIMIT)


def _dot(a, b):
    return jnp.dot(a, b, preferred_element_type=F32)


def _dot_nt(a, b):
    return lax.dot_general(a, b, (((1,), (1,)), ((), ())), preferred_element_type=F32)


def _dot_tn(a, b):
    return lax.dot_general(a, b, (((0,), (0,)), ((), ())), preferred_element_type=F32)


def _split(x):
    hi = x.astype(BF16)
    lo = (x - hi.astype(F32)).astype(BF16)
    return hi, lo


def _dot3(a, b, dot=_dot):
    m = a[0].shape[0]
    both = dot(jnp.concatenate([a[0], a[1]], axis=0), b[0])
    return both[0:m] + both[m:] + dot(a[0], b[1])


def _dot3_tn(a, b):
    m = a[0].shape[1]
    both = _dot_tn(jnp.concatenate([a[0], a[1]], axis=1), b[0])
    return both[0:m] + both[m:] + _dot_tn(a[0], b[1])


def _group_sum(x, bd):
    outs = []
    for g in range(x.shape[1] // LANES):
        hi, lo = _split(x[:, g * LANES:(g + 1) * LANES])
        outs.append(_dot(hi, bd) + _dot(lo, bd))
    return outs[0] if len(outs) == 1 else jnp.concatenate(outs, axis=1)


def _rope(x, cos, sin, half):
    lane = lax.broadcasted_iota(jnp.int32, x.shape, 1)
    up = pltpu.roll(x, LANES - half, axis=1)
    dn = pltpu.roll(x, half, axis=1)
    sw = jnp.where((lane % (2 * half)) < half, up, dn)
    return x * cos + sw * sin


def _rms(x, gain):
    ms = jnp.mean(x * x, axis=-1, keepdims=True)
    return x * lax.rsqrt(ms + NORM_EPS) * gain


def _mod_kernel(c_ref, w_ref, b_ref, o_ref):
    c = c_ref[...]
    s = (c * jax.nn.sigmoid(c)).astype(BF16)
    o_ref[...] = _dot(s, w_ref[...].astype(BF16)) + b_ref[...]


def _mod_call(c_all, w_mod, b_mod):
    n_layers, d, n = w_mod.shape
    bn = 1536
    return pl.pallas_call(
        _mod_kernel,
        out_shape=jax.ShapeDtypeStruct((n_layers, 8, n), F32),
        grid=(n_layers, n // bn),
        in_specs=[
            pl.BlockSpec((8, d), lambda l, j: (0, 0)),
            pl.BlockSpec((None, d, bn), lambda l, j: (l, 0, j)),
            pl.BlockSpec((None, 1, bn), lambda l, j: (l, 0, j)),
        ],
        out_specs=pl.BlockSpec((None, 8, bn), lambda l, j: (l, 0, j)),
        compiler_params=_cparams(2),
        name="adaln_mod",
    )(c_all, w_mod, b_mod.reshape(n_layers, 1, n))


def _stream_kernel(body, n_src, n_tiles, n_ctx_tiles):
    def kern(*refs):
        if n_src == 1:
            x = refs[0][...]
        else:
            is_ctx = pl.program_id(0) % n_tiles < n_ctx_tiles
            x = jnp.where(is_ctx, refs[0][...], refs[1][...])
        body(x, *refs[n_src:])
    return kern


def _stream_specs(xs, n_tiles, n_ctx_tiles):
    if not isinstance(xs, tuple):
        return (xs,), [_rows(D_MODEL)]
    n_lat = n_tiles - n_ctx_tiles
    ctx_spec = pl.BlockSpec((ROW_TILE, D_MODEL), lambda i: (
        (i // n_tiles) * n_ctx_tiles + jnp.minimum(i % n_tiles, n_ctx_tiles - 1), 0))
    lat_spec = pl.BlockSpec((ROW_TILE, D_MODEL), lambda i: (
        (i // n_tiles) * n_lat + jnp.maximum(i % n_tiles - n_ctx_tiles, 0), 0))
    return xs, [ctx_spec, lat_spec]


def _inproj_body(x, g1_ref, sh_ref, sc_ref, w_ref, wkr_ref, gain_ref, bd_ref,
                 gc_ref, gs_ref, mc_ref, ms_ref,
                 qn_ref, qup_ref, kvn_ref, kvk_ref, kvv_ref,
                 gq_ref, gk_ref, gv_ref, ru_ref, mq_ref, mk_ref, mv_ref):
    h = (_rms(x, g1_ref[...]) * (1.0 + sc_ref[...]) + sh_ref[...]).astype(BF16)
    p = _dot(h, w_ref[...])

    bd = bd_ref[...]
    gc, gs = gc_ref[...], gs_ref[...]
    n_q_slabs = GQA_HEADS * GQA_HEAD_DIM // LANES
    n_k_slabs = GQA_KV_HEADS * GQA_HEAD_DIM // LANES
    for g in range(n_q_slabs + n_k_slabs):
        slab = p[:, g * LANES:(g + 1) * LANES]
        ms = _group_sum(slab * slab, bd) * (1.0 / GQA_HEAD_DIM)
        y = slab * lax.rsqrt(ms + NORM_EPS) * gain_ref[:, g * LANES:(g + 1) * LANES]
        y = _rope(y, gc, gs, GQA_HEAD_DIM // 4)
        if g < n_q_slabs:
            gq_ref[:, g * LANES:(g + 1) * LANES] = (y * (GQA_SCALE * LOG2_E)).astype(BF16)
        else:
            gk_ref[:, (g - n_q_slabs) * LANES:(g - n_q_slabs + 1) * LANES] = y.astype(BF16)
    gv_ref[...] = p[:, _C_GV:_C_RW].astype(BF16).T

    ru_ref[...] = p[:, _C_RW:_C_QD]

    mc, msn = mc_ref[...], ms_ref[...]
    qd = _rms(p[:, _C_QD:_C_KVD], qn_ref[...])
    q = _dot(qd.astype(BF16), qup_ref[...])
    kvd = _rms(p[:, _C_KVD:_C_KR], kvn_ref[...]).astype(BF16)
    kn = _dot(kvd, kvk_ref[...])
    kr = _rope(_dot(h, wkr_ref[...]), mc, msn, QK_ROPE // 4)
    for hd in range(MLA_HEADS):
        sl = slice(hd * MLA_SLOT, (hd + 1) * MLA_SLOT)
        mq_ref[:, sl] = (_rope(q[:, sl], mc, msn, QK_ROPE // 4) * (MLA_SCALE * LOG2_E)).astype(BF16)
        mk_ref[:, sl] = (kn[:, sl] + kr).astype(BF16)
    mv_ref[...] = _dot(kvd, kvv_ref[...]).astype(BF16).T


def _row_mod_spec(which, n_tiles, n_ctx_tiles, ctx_row):
    def index(i):
        b, t = i // n_tiles, i % n_tiles
        return (jnp.where(t < n_ctx_tiles, ctx_row, b), which, 0, 0)
    return pl.BlockSpec((None, None, 1, D_MODEL), index)


def _full(shape):
    return pl.BlockSpec(shape, lambda i: (0,) * len(shape))


def _rows(width):
    return pl.BlockSpec((ROW_TILE, width), lambda i: (i, 0))


def _inproj_call(xs, mod, g1, w_main, w_kr, gains, bd, tabs, qn, qup, kvn, kvk, kvv, geom):
    n_rows, n_tiles, n_ctx_tiles, ctx_row = geom
    tab_spec = pl.BlockSpec((ROW_TILE, LANES), lambda i: (i % n_tiles, 0))
    widths = (GQA_HEADS * GQA_HEAD_DIM, GQA_KV_HEADS * GQA_HEAD_DIM, GQA_KV_HEADS * GQA_HEAD_DIM,
              RWKV_IN, MLA_HEADS * MLA_SLOT, MLA_HEADS * MLA_SLOT, MLA_HEADS * V_HEAD)
    dtypes = (BF16, BF16, BF16, F32, BF16, BF16, BF16)
    transposed = (False, False, True, False, False, False, True)
    n_batch = n_rows // (n_tiles * ROW_TILE)
    out_shapes = [jax.ShapeDtypeStruct((n_batch, w, n_tiles * ROW_TILE) if tr else (n_rows, w), dt)
                  for w, dt, tr in zip(widths, dtypes, transposed)]
    out_specs = [pl.BlockSpec((None, w, ROW_TILE), lambda i: (i // n_tiles, 0, i % n_tiles)) if tr else _rows(w)
                 for w, tr in zip(widths, transposed)]
    srcs, src_specs = _stream_specs(xs, n_tiles, n_ctx_tiles)
    return pl.pallas_call(
        _stream_kernel(_inproj_body, len(srcs), n_tiles, n_ctx_tiles),
        out_shape=out_shapes,
        grid=(n_rows // ROW_TILE,),
        in_specs=[
            *src_specs, _full((1, D_MODEL)),
            _row_mod_spec(0, n_tiles, n_ctx_tiles, ctx_row),
            _row_mod_spec(1, n_tiles, n_ctx_tiles, ctx_row),
            _full(w_main.shape), _full(w_kr.shape), _full(gains.shape), _full(bd.shape),
            tab_spec, tab_spec, tab_spec, tab_spec,
            _full(qn.shape), _full(qup.shape), _full(kvn.shape), _full(kvk.shape), _full(kvv.shape),
        ],
        out_specs=out_specs,
        compiler_params=_cparams(1),
        name="in_proj",
    )(*srcs, g1, mod, mod, w_main, w_kr, gains, bd, *tabs, qn, qup, kvn, kvk, kvv)


def _attn_kernel(q_ref, k_ref, vt_ref, o_ref, s_ref, *, heads, group, dq, dv, n_ctx_tiles, ctx_len, s_tot):
    t = pl.program_id(1)
    sub = 8
    width = LANES // dv
    assert heads % width == 0

    def run(nk):
        blocks = [(lo, min(lo + ATTN_KEY_BLOCK, nk)) for lo in range(0, nk, ATTN_KEY_BLOCK)]

        def score_block(hq, lo, hi, m8):
            hk = hq // group
            sj = _dot_nt(k_ref[lo:hi, hk * dq:(hk + 1) * dq], q_ref[:, hq * dq:(hq + 1) * dq])
            s_ref[hq % (2 * width), lo:hi, :] = sj
            mj = jnp.max(sj.reshape((hi - lo) // sub, sub, ROW_TILE), axis=0)
            return mj if m8 is None else jnp.maximum(m8, mj)

        def prob_block(hq, lo, hi, m, l8, acc):
            hk = hq // group
            e = jnp.exp2(s_ref[hq % (2 * width), lo:hi, :] - m)
            lj = jnp.sum(e.reshape((hi - lo) // sub, sub, ROW_TILE), axis=0)
            pj = _dot(vt_ref[hk * dv:(hk + 1) * dv, lo:hi], e.astype(BF16))
            return (lj if l8 is None else l8 + lj), (pj if acc is None else acc + pj)

        m8 = [None] * width
        for lo, hi in blocks:
            for w in range(width):
                m8[w] = score_block(w, lo, hi, m8[w])
        for h0 in range(0, heads, width):
            m = [jnp.max(m8[w], axis=0, keepdims=True) for w in range(width)]
            m8, l8, acc = [None] * width, [None] * width, [None] * width
            for lo, hi in blocks:
                for w in range(width):
                    if h0 + width + w < heads:
                        m8[w] = score_block(h0 + width + w, lo, hi, m8[w])
                    l8[w], acc[w] = prob_block(h0 + w, lo, hi, m[w], l8[w], acc[w])
            out_t = [acc[w] / jnp.sum(l8[w], axis=0, keepdims=True) for w in range(width)]
            o_ref[:, h0 * dv:(h0 + width) * dv] = jnp.concatenate(out_t, axis=0).T.astype(o_ref.dtype)

    @pl.when(t < n_ctx_tiles)
    def _():
        run(ctx_len)

    @pl.when(t >= n_ctx_tiles)
    def _():
        run(s_tot)


def _attn_call(q, k, vt, *, heads, group, dq, dv, batch, s_tot, ctx_len, name):
    n_tiles = s_tot // ROW_TILE
    assert ctx_len % LANES == 0 and s_tot % LANES == 0
    kern = functools.partial(_attn_kernel, heads=heads, group=group, dq=dq, dv=dv,
                             n_ctx_tiles=ctx_len // ROW_TILE, ctx_len=ctx_len, s_tot=s_tot)
    return pl.pallas_call(
        kern,
        out_shape=jax.ShapeDtypeStruct((batch * s_tot, heads * dv), BF16),
        grid=(batch, n_tiles),
        in_specs=[
            pl.BlockSpec((ROW_TILE, q.shape[1]), lambda b, t: (b * n_tiles + t, 0)),
            pl.BlockSpec((s_tot, k.shape[1]), lambda b, t: (b, 0)),
            pl.BlockSpec((None, vt.shape[1], s_tot), lambda b, t: (b, 0, 0)),
        ],
        out_specs=pl.BlockSpec((ROW_TILE, heads * dv), lambda b, t: (b * n_tiles + t, 0)),
        scratch_shapes=[pltpu.VMEM((2 * (LANES // dv), s_tot, ROW_TILE), F32)],
        compiler_params=_cparams(2),
        name=name,
    )(q, k, vt)


def _rwkv_prep_kernel(u_ref, prev_ref, next_ref, mu_ref, kk_ref, ka_ref, rk_ref, w0_ref, w2_ref,
                      a0_ref, a2_ref, g2_ref, bd_ref, tri_ref,
                      v_ref, bonus_ref, gate_ref, lhs0_ref, rhs0_ref, lhs1_ref, rhs1_ref, pc_ref,
                      *, n_tiles, n_ctx_tiles):
    t = pl.program_id(0) % n_tiles
    u = u_ref[...]
    rows = u.shape[0]
    sub = 8
    row = lax.broadcasted_iota(jnp.int32, (sub, u.shape[1]), 0)
    seq_start = jnp.logical_or(t == 0, t == n_ctx_tiles)
    seq_end = jnp.logical_or(t == n_ctx_tiles - 1, t == n_tiles - 1)
    halo_prev = prev_ref[sub - 1:sub, :] * jnp.where(seq_start, 0.0, 1.0)
    halo_next = next_ref[0:1, :] * jnp.where(seq_end, 0.0, 1.0)
    prev = pltpu.roll(u, 1, axis=0)
    nxt = pltpu.roll(u, rows - 1, axis=0)
    prev = jnp.concatenate([jnp.where(row == 0, halo_prev, prev[0:sub]), prev[sub:]], axis=0)
    nxt = jnp.concatenate([nxt[0:rows - sub], jnp.where(row == sub - 1, halo_next, nxt[rows - sub:])], axis=0)
    mu_p, mu_n = mu_ref[0:1, :], mu_ref[1:2, :]
    us = u * (1.0 - mu_p - mu_n) + mu_p * prev + mu_n * nxt

    c = RWKV_DIM
    r, k, v = us[:, 0:c], us[:, c:2 * c], us[:, 2 * c:3 * c]
    wd = jnp.tanh(us[:, 3 * c:3 * c + LANES]).astype(BF16)
    ad = us[:, 3 * c + LANES:3 * c + 2 * LANES].astype(BF16)
    gd = us[:, 3 * c + 2 * LANES:3 * c + 3 * LANES]
    bd = bd_ref[...]

    kk = k * kk_ref[...]
    kk = kk / jnp.maximum(jnp.sqrt(_group_sum(kk * kk, bd)), 1e-12)

    v_ref[...] = v.astype(BF16)
    gate_ref[...] = _dot(jax.nn.sigmoid(gd).astype(BF16), g2_ref[...])

    outs = ((lhs0_ref, rhs0_ref), (lhs1_ref, rhs1_ref))
    n_chunks = rows // CHUNK
    bonus = None
    for d in range(2):
        z = w0_ref[d:d + 1, :] + _dot(wd, w2_ref[d])
        nz = -z
        softplus = jnp.maximum(nz, 0.0) + jnp.log(1.0 + jnp.exp(-jnp.abs(nz)))
        lw = -jnp.exp(-softplus - 0.5)
        a = jax.nn.sigmoid(a0_ref[d:d + 1, :] + _dot(ad, a2_ref[d]))
        k_d = k * (1.0 + (a - 1.0) * ka_ref[...])
        b_d = _group_sum(r * k_d * rk_ref[...], bd) * v
        bonus = b_d if bonus is None else bonus + b_d

        tri = tri_ref[d]
        hi = lw.astype(BF16)
        rem = lw - hi.astype(F32)
        mid = rem.astype(BF16)
        lo = (rem - mid.astype(F32)).astype(BF16)
        cum = _dot(tri, hi) + (_dot(tri, mid) + _dot(tri, lo))
        e_neg = jnp.exp(-cum)
        lhs_ref, rhs_ref = outs[d]
        rt = (r * jnp.exp(cum)).astype(BF16)
        at = (-kk * jnp.exp(cum - lw)).astype(BF16)
        kt = (k_d * e_neg).astype(BF16)
        bt = (kk * a * e_neg).astype(BF16)
        for ci in range(n_chunks):
            rs = slice(ci * CHUNK, (ci + 1) * CHUNK)
            lhs_ref[ci, 0:CHUNK, :] = at[rs]
            lhs_ref[ci, CHUNK:, :] = rt[rs]
            rhs_ref[ci, 0:CHUNK, :] = bt[rs]
            rhs_ref[ci, CHUNK:, :] = kt[rs]
        last = CHUNK - 1 if d == 0 else 0
        tot = jnp.concatenate([cum[ci * CHUNK + last:ci * CHUNK + last + 1, :] for ci in range(n_chunks)], axis=0)
        pc_ref[d * n_chunks:(d + 1) * n_chunks, :] = jnp.exp(tot)
    bonus_ref[...] = bonus


def _rwkv_prep_call(ru, mu, kk, ka, rk, w0, w2p, a0, a2p, g2, bd, tri, geom):
    n_rows, n_tiles, n_ctx_tiles, _ = geom
    grid = n_rows // ROW_TILE
    g8 = ROW_TILE // 8
    last8 = n_rows // 8 - 1
    kern = functools.partial(_rwkv_prep_kernel, n_tiles=n_tiles, n_ctx_tiles=n_ctx_tiles)
    c = RWKV_DIM
    n_pc = 2 * (ROW_TILE // CHUNK)
    return pl.pallas_call(
        kern,
        out_shape=[jax.ShapeDtypeStruct((n_rows, c), BF16)] + [jax.ShapeDtypeStruct((n_rows, c), F32)] * 2
                  + [jax.ShapeDtypeStruct((n_rows // CHUNK, 2 * CHUNK, c), BF16)] * 4
                  + [jax.ShapeDtypeStruct((grid * n_pc, c), F32)],
        grid=(grid,),
        in_specs=[
            _rows(RWKV_IN),
            pl.BlockSpec((8, RWKV_IN), lambda i: (jnp.maximum(i * g8 - 1, 0), 0)),
            pl.BlockSpec((8, RWKV_IN), lambda i: (jnp.minimum((i + 1) * g8, last8), 0)),
            _full(mu.shape), _full(kk.shape), _full(ka.shape), _full(rk.shape),
            _full(w0.shape), _full(w2p.shape), _full(a0.shape), _full(a2p.shape),
            _full(g2.shape), _full(bd.shape), _full(tri.shape),
        ],
        out_specs=[_rows(c)] * 3
                  + [pl.BlockSpec((ROW_TILE // CHUNK, 2 * CHUNK, c), lambda i: (i, 0, 0))] * 4
                  + [pl.BlockSpec((n_pc, c), lambda i: (i, 0))],
        compiler_params=_cparams(1),
        name="rwkv_prep",
    )(ru, ru, ru, mu, kk, ka, rk, w0, w2p, a0, a2p, g2, bd, tri)


def _rwkv_scan_kernel(*refs, nb):
    ins, (y0_ref, y1_ref, s_ref) = refs[:8], refs[8:]
    y_refs = (y0_ref, y1_ref)

    @pl.when(pl.program_id(1) == 0)
    def _():
        s_ref[...] = jnp.zeros_like(s_ref)

    n = RWKV_HEAD
    c = CHUNK
    ri = lax.broadcasted_iota(jnp.int32, (2 * c, 2 * c), 0)
    ci = lax.broadcasted_iota(jnp.int32, (2 * c, 2 * c), 1)
    tq, ts = ri % c, ci % c
    masks = []
    for d in range(2):
        earlier = (ts < tq) if d == 0 else (ts > tq)
        masks.append(jnp.where(jnp.logical_or(earlier, jnp.logical_and(ri >= c, ts == tq)), 1.0, 0.0))

    chains = [(b, d, hd) for b in range(nb) for d in range(2) for hd in range(RWKV_HEADS)]
    bf = lambda t: t.astype(BF16)

    def head(arr, hd):
        return arr[:, hd * n:(hd + 1) * n]

    lhs_h = [head(ins[4 * d][b], hd) for b, d, hd in chains]
    rhs_h = [head(ins[4 * d + 1][b], hd) for b, d, hd in chains]
    v_h = [head(ins[4 * d + 2][b], hd) for b, d, hd in chains]
    pc_h = [head(ins[4 * d + 3][b], hd) for b, d, hd in chains]
    s_h = [head(s_ref[b, d], hd) for b, d, hd in chains]
    gram = [_dot_nt(l_, r_) * masks[d] for (_, d, _), l_, r_ in zip(chains, lhs_h, rhs_h)]
    ls = [_dot_nt(l_, bf(si)) for l_, si in zip(lhs_h, s_h)]
    zeros = jnp.zeros((c, n), BF16)
    xin = [a[0:c] + _dot(bf(g[0:c]), jnp.concatenate([zeros, vi], axis=0))
           for a, g, vi in zip(ls, gram, v_h)]
    right = lax.broadcasted_iota(jnp.int32, (c, 2 * n), 1) >= n
    rj = [jnp.concatenate([g[0:c, 0:n], xi], axis=1) for g, xi in zip(gram, xin)]
    n_levels = c.bit_length() - 1
    for level in range(n_levels):
        nxt = []
        for r_ in rj:
            hi = bf(r_)
            if level < n_levels - NEUMANN_SINGLE_PASS_LEVELS:
                lo = bf(r_ - hi.astype(F32))
                both = _dot(jnp.concatenate([hi[:, 0:n], lo[:, 0:n]], axis=0), hi)
                p = both[0:c] + both[c:] + _dot(hi[:, 0:n], lo)
            else:
                p = _dot(hi[:, 0:n], hi)
            nxt.append(p + jnp.where(right, r_, 0.0))
        rj = nxt
    uv = [jnp.concatenate([bf(r_[:, n:]), vi], axis=0) for r_, vi in zip(rj, v_h)]
    y = [a[c:] + _dot(bf(g[c:]), uvi) for a, g, uvi in zip(ls, gram, uv)]
    s_new = [(si + _dot_tn(uvi, r_)) * pci for si, uvi, r_, pci in zip(s_h, uv, rhs_h, pc_h)]
    per = RWKV_HEADS
    for b in range(nb):
        for d in range(2):
            first = (b * 2 + d) * per
            y_refs[d][b] = jnp.concatenate(y[first:first + per], axis=1)
            s_ref[b, d] = jnp.concatenate(s_new[first:first + per], axis=1)


def _rwkv_scan_call(prep, batch, s_tot, ctx_len):
    v, _, _, lhs0, rhs0, lhs1, rhs1, pc = prep
    n_ch = s_tot // CHUNK
    n_ctx_ch = ctx_len // CHUNK
    per_tile = ROW_TILE // CHUNK
    c = RWKV_DIM
    nb = SCAN_BATCH if batch % SCAN_BATCH == 0 else 1

    def chunk_of(d, i):
        if d == 0:
            return i
        return jnp.where(i < n_ctx_ch, n_ctx_ch - 1 - i, n_ctx_ch + n_ch - 1 - i)

    def stacked_spec(d):
        return pl.BlockSpec((nb, None, 2 * CHUNK, c), lambda b, i: (b, chunk_of(d, i), 0, 0))

    def row_spec(d):
        return pl.BlockSpec((nb, CHUNK, c), lambda b, i: (b, chunk_of(d, i), 0))

    def pc_spec(d):
        def index(b, i):
            ch = chunk_of(d, i)
            return (b, (ch // per_tile) * 2 * per_tile + d * per_tile + ch % per_tile, 0, 0)
        return pl.BlockSpec((nb, None, 1, c), index)

    v3 = v.reshape(batch, s_tot, c)
    pc4 = pc.reshape(batch, pc.shape[0] // batch, 1, c)
    ins, specs = [], []
    for d, (lhs, rhs) in enumerate(((lhs0, rhs0), (lhs1, rhs1))):
        ins += [lhs.reshape(batch, n_ch, 2 * CHUNK, c), rhs.reshape(batch, n_ch, 2 * CHUNK, c), v3, pc4]
        specs += [stacked_spec(d), stacked_spec(d), row_spec(d), pc_spec(d)]
    y0, y1 = pl.pallas_call(
        functools.partial(_rwkv_scan_kernel, nb=nb),
        out_shape=[jax.ShapeDtypeStruct((batch, s_tot, c), F32)] * 2,
        grid=(batch // nb, n_ch),
        in_specs=specs,
        out_specs=[row_spec(0), row_spec(1)],
        scratch_shapes=[pltpu.VMEM((nb, 2, RWKV_HEAD, c), F32)],
        compiler_params=_cparams(2),
        name="rwkv_scan",
    )(*ins)
    return y0.reshape(batch * s_tot, c), y1.reshape(batch * s_tot, c)


def _merge_body(x, g1_ref, sh_ref, sc_ref, gt_ref, a_ref, y0_ref, y1_ref, bonus_ref, rg_ref, m_ref,
                lnw_ref, lnb_ref, bd_ref, wg_ref, wb_ref, wo_ref, o_ref):
    h = (_rms(x, g1_ref[...]) * (1.0 + sc_ref[...]) + sh_ref[...]).astype(BF16)

    bd = bd_ref[...]
    y = y0_ref[...] + y1_ref[...]
    mu = _group_sum(y, bd) * (1.0 / RWKV_HEAD)
    yc = y - mu
    var = _group_sum(yc * yc, bd) * (1.0 / RWKV_HEAD)
    yn = yc * lax.rsqrt(var + LNX_EPS) * lnw_ref[...] + lnb_ref[...]
    r_out = ((yn + bonus_ref[...]) * rg_ref[...]).astype(BF16)

    d = D_MODEL
    mixed = None
    for i, yb in enumerate((a_ref[...], r_out, m_ref[...])):
        gate = jax.nn.sigmoid(_dot(h, wg_ref[:, i * d:(i + 1) * d]))
        term = gate * _dot(yb, wb_ref[i])
        mixed = term if mixed is None else mixed + term
    o_ref[...] = x + gt_ref[...] * _dot(mixed.astype(BF16), wo_ref[...])


def _merge_call(xs, mod, g1, a, y0, y1, bonus, rg, m, lnw, lnb, bd, wg, wb, wo, geom, latent_only):
    n_rows, n_tiles, n_ctx_tiles, ctx_row = geom
    c = RWKV_DIM
    if latent_only:
        n_lat = n_tiles - n_ctx_tiles
        n_out_tiles = (n_rows // ROW_TILE) // n_tiles * n_lat
        src_tile = lambda i: (i // n_lat) * n_tiles + n_ctx_tiles + i % n_lat
        ms = lambda which: _row_mod_spec(which, n_lat, 0, ctx_row)
    else:
        n_out_tiles = n_rows // ROW_TILE
        src_tile = lambda i: i
        ms = lambda which: _row_mod_spec(which, n_tiles, n_ctx_tiles, ctx_row)
    rows_in = lambda width: pl.BlockSpec((ROW_TILE, width), lambda i: (src_tile(i), 0))
    if isinstance(xs, tuple):
        assert not latent_only
        srcs, src_specs = _stream_specs(xs, n_tiles, n_ctx_tiles)
    else:
        srcs, src_specs = (xs,), [rows_in(D_MODEL)]
    return pl.pallas_call(
        _stream_kernel(_merge_body, len(srcs), n_tiles, n_ctx_tiles),
        out_shape=jax.ShapeDtypeStruct((n_out_tiles * ROW_TILE, D_MODEL), F32),
        grid=(n_out_tiles,),
        in_specs=[
            *src_specs, _full((1, D_MODEL)), ms(0), ms(1), ms(2),
            rows_in(c), rows_in(c), rows_in(c), rows_in(c), rows_in(c), rows_in(c),
            _full(lnw.shape), _full(lnb.shape), _full(bd.shape),
            _full(wg.shape), _full(wb.shape), _full(wo.shape),
        ],
        out_specs=_rows(D_MODEL),
        compiler_params=_cparams(1),
        name="merge",
    )(*srcs, g1, mod, mod, mod, a, y0, y1, bonus, rg, m, lnw, lnb, bd, wg, wb, wo)


def _mlp_kernel(x_ref, g2_ref, sh_ref, sc_ref, gt_ref, w1_ref, w2_ref, gf_ref, o_ref, *, final_norm):
    x = x_ref[...]
    h = (_rms(x, g2_ref[...]) * (1.0 + sc_ref[...]) + sh_ref[...]).astype(BF16)
    acc = None
    blk = D_MODEL
    for j in range(D_FF // blk):
        a = jnp.maximum(_dot(h, w1_ref[:, j * blk:(j + 1) * blk]), 0.0)
        part = _dot((a * a).astype(BF16), w2_ref[j * blk:(j + 1) * blk, :])
        acc = part if acc is None else acc + part
    out = x + gt_ref[...] * acc
    if final_norm:
        out = _rms(out, gf_ref[...])
    o_ref[...] = out


def _mlp_call(xs, mod, g2, w1, w2, gf, geom, final_norm):
    n_rows, n_tiles, n_ctx_tiles, ctx_row = geom
    ms = lambda which: _row_mod_spec(which, n_tiles, n_ctx_tiles, ctx_row)
    return pl.pallas_call(
        functools.partial(_mlp_kernel, final_norm=final_norm),
        out_shape=jax.ShapeDtypeStruct((n_rows, D_MODEL), F32),
        grid=(n_rows // ROW_TILE,),
        in_specs=[
            _rows(D_MODEL), _full((1, D_MODEL)), ms(3), ms(4), ms(5),
            _full(w1.shape), _full(w2.shape), _full((1, D_MODEL)),
        ],
        out_specs=_rows(D_MODEL),
        compiler_params=_cparams(1),
        name="mlp",
    )(xs, g2, mod, mod, mod, w1, w2, gf)


def _rope_tables(ctx_len, seq):
    pos = jnp.arange(seq, dtype=jnp.int32)
    rows, cols = pos // GRID_W, pos % GRID_W

    def cs(p, n):
        inv = ROPE_THETA ** (-jnp.arange(n, dtype=F32) / n)
        ang = p.astype(F32)[:, None] * inv[None, :]
        return jnp.cos(ang), jnp.sin(ang)

    def axial(n):
        cr, sr = cs(rows, n)
        cc, sc = cs(cols, n)
        return jnp.concatenate([cr, cr, cc, cc], -1), jnp.concatenate([-sr, sr, -sc, sc], -1)

    def with_ctx(cos, sin):
        w = cos.shape[1]
        return (jnp.concatenate([jnp.ones((ctx_len, w), F32), cos], 0),
                jnp.concatenate([jnp.zeros((ctx_len, w), F32), sin], 0))

    gc, gs = axial(GQA_HEAD_DIM // 4)
    gc, gs = jnp.tile(gc, (1, 2)), jnp.tile(gs, (1, 2))
    mc, ms = axial(QK_ROPE // 4)
    pad = MLA_SLOT - QK_NOPE - QK_ROPE
    mc = jnp.concatenate([jnp.ones((seq, QK_NOPE), F32), mc, jnp.ones((seq, pad), F32)], -1)
    ms = jnp.concatenate([jnp.zeros((seq, QK_NOPE), F32), ms, jnp.zeros((seq, pad), F32)], -1)
    return with_ctx(gc, gs) + with_ctx(mc, ms)


def _block_ones(n, blk):
    i = jnp.arange(n)
    return (i[:, None] // blk == i[None, :] // blk)


def _chunk_tri():
    i = jnp.arange(ROW_TILE)
    same = _block_ones(ROW_TILE, CHUNK)
    fwd = jnp.logical_and(same, i[None, :] <= i[:, None])
    bwd = jnp.logical_and(same, i[None, :] >= i[:, None])
    return jnp.stack([fwd, bwd]).astype(BF16)


def _pack_layer(l, w_in, gqa_q_gain, gqa_k_gain, mla_q_up, mla_kv_up, rwkv_w2, rwkv_a2):
    w = w_in[l]
    o_kr = _C_KR
    o_gate = o_kr + QK_ROPE
    w_main = w[:, 0:o_kr]
    w_kr = jnp.pad(w[:, o_kr:o_gate], ((0, 0), (QK_NOPE, MLA_SLOT - QK_NOPE - QK_ROPE)))
    w_gate = w[:, o_gate:]
    gains = jnp.concatenate([jnp.tile(gqa_q_gain[l], GQA_HEADS), jnp.tile(gqa_k_gain[l], GQA_KV_HEADS)])[None, :]

    qup = mla_q_up[l].reshape(Q_LORA, MLA_HEADS, QK_NOPE + QK_ROPE)
    qup = jnp.pad(qup, ((0, 0), (0, 0), (0, MLA_SLOT - QK_NOPE - QK_ROPE))).reshape(Q_LORA, MLA_HEADS * MLA_SLOT)
    kvu = mla_kv_up[l].reshape(KV_LORA, MLA_HEADS, QK_NOPE + V_HEAD)
    kvk = jnp.pad(kvu[:, :, :QK_NOPE], ((0, 0), (0, 0), (0, MLA_SLOT - QK_NOPE))).reshape(KV_LORA, MLA_HEADS * MLA_SLOT)
    kvv = kvu[:, :, QK_NOPE:].reshape(KV_LORA, MLA_HEADS * V_HEAD)

    def pad_dir(w2):
        lora = w2.shape[1]
        z = jnp.zeros_like(w2[0])
        return jnp.stack([jnp.concatenate([w2[0], z], 0), jnp.concatenate([z, w2[1]], 0)]).astype(BF16)

    return dict(w_main=w_main, w_kr=w_kr, w_gate=w_gate, gains=gains, qup=qup, kvk=kvk, kvv=kvv,
                w2p=pad_dir(rwkv_w2[l]), a2p=pad_dir(rwkv_a2[l]))


def kernel(x, c, ctx, c_ctx, w_mod, b_mod, g_norm1, g_norm2, w_in, gqa_q_gain, gqa_k_gain, rwkv_shift_mu, rwkv_w0, rwkv_w2, rwkv_a0, rwkv_a2, rwkv_g2, rwkv_k_k, rwkv_k_a, rwkv_r_k, rwkv_ln_w, rwkv_ln_b, mla_q_norm, mla_q_up, mla_kv_norm, mla_kv_up, w_branch, w_out, w_ff1, w_ff2, g_final):
    batch, seq, d = x.shape
    ctx_len = ctx.shape[1]
    depth = w_mod.shape[0]
    s_tot = ctx_len + seq
    assert d == D_MODEL and batch < 8
    assert ctx_len % ROW_TILE == 0 and seq % ROW_TILE == 0 and seq % GRID_W == 0
    n_rows = batch * s_tot
    geom = (n_rows, s_tot // ROW_TILE, ctx_len // ROW_TILE, batch)

    c_all = jnp.concatenate([c, c_ctx[None, :], jnp.zeros((8 - batch - 1, d), F32)], axis=0)
    mod_all = _mod_call(c_all, w_mod, b_mod).reshape(depth, 8, 6, 1, d)

    tabs = _rope_tables(ctx_len, seq)
    bd = _block_ones(LANES, RWKV_HEAD).astype(BF16)
    tri = _chunk_tri()
    row1 = lambda v: v.reshape(1, -1)

    w_in, mla_q_up, mla_kv_up, rwkv_g2, w_branch, w_out, w_ff1, w_ff2 = (
        w.astype(BF16) for w in (w_in, mla_q_up, mla_kv_up, rwkv_g2, w_branch, w_out, w_ff1, w_ff2))

    xs = (ctx.reshape(batch * ctx_len, d), x.reshape(batch * seq, d))
    if depth == 1:
        xs = jnp.concatenate([ctx, x], axis=1).reshape(n_rows, d)
    for l in range(depth):
        pk = _pack_layer(l, w_in, gqa_q_gain, gqa_k_gain, mla_q_up, mla_kv_up, rwkv_w2, rwkv_a2)
        mod = mod_all[l]
        g1 = row1(g_norm1[l])
        gq, gk, gv, ru, mq, mk, mv = _inproj_call(
            xs, mod, g1, pk["w_main"], pk["w_kr"], pk["gains"], bd, tabs,
            row1(mla_q_norm[l]), pk["qup"], row1(mla_kv_norm[l]), pk["kvk"], pk["kvv"], geom)
        a_out = _attn_call(gq, gk, gv, heads=GQA_HEADS, group=GQA_HEADS // GQA_KV_HEADS, dq=GQA_HEAD_DIM,
                           dv=GQA_HEAD_DIM, batch=batch, s_tot=s_tot, ctx_len=ctx_len, name="gqa_attn")
        m_out = _attn_call(mq, mk, mv, heads=MLA_HEADS, group=1, dq=MLA_SLOT, dv=V_HEAD,
                           batch=batch, s_tot=s_tot, ctx_len=ctx_len, name="mla_attn")
        prep = _rwkv_prep_call(ru, rwkv_shift_mu[l], row1(rwkv_k_k[l]), row1(rwkv_k_a[l]), row1(rwkv_r_k[l]),
                               rwkv_w0[l], pk["w2p"], rwkv_a0[l], pk["a2p"], rwkv_g2[l], bd, tri, geom)
        y0, y1 = _rwkv_scan_call(prep, batch, s_tot, ctx_len)
        last = l == depth - 1
        xs = _merge_call(xs, mod, g1, a_out, y0, y1, prep[1], prep[2], m_out,
                         row1(rwkv_ln_w[l]), row1(rwkv_ln_b[l]), bd, pk["w_gate"],
                         w_branch[l], w_out[l], geom, latent_only=last)
        mlp_geom = (batch * seq, seq // ROW_TILE, 0, batch) if last else geom
        xs = _mlp_call(xs, mod, row1(g_norm2[l]), w_ff1[l], w_ff2[l],
                       row1(g_final), mlp_geom, final_norm=last)
    return xs.reshape(batch, seq, d)
```

```python
import functools

import jax
import jax.numpy as jnp
from jax import lax
from jax.experimental import pallas as pl
from jax.experimental.pallas import tpu as pltpu

F32 = jnp.float32
BF16 = jnp.bfloat16

D_MODEL = 1024
GRID_W = 64
ROPE_THETA = 10000.0
NORM_EPS = 1e-6

GQA_HEADS = 8
GQA_KV_HEADS = 2
GQA_HEAD_DIM = 64
GQA_SCALE = GQA_HEAD_DIM ** -0.5

RWKV_HEADS = 8
RWKV_HEAD = 64
RWKV_DIM = RWKV_HEADS * RWKV_HEAD
DECAY_LORA = 64
AAA_LORA = 64
GATE_LORA = 128
LNX_EPS = 64e-5
RWKV_IN = 3 * RWKV_DIM + 2 * DECAY_LORA + 2 * AAA_LORA + GATE_LORA

MLA_HEADS = 8
Q_LORA = 384
KV_LORA = 256
QK_NOPE = 64
QK_ROPE = 32
V_HEAD = 64
MLA_SCALE = (QK_NOPE + QK_ROPE) ** -0.5
MLA_SLOT = 128
ATTN_KEY_BLOCK = 512
LOG2_E = 1.4426950408889634

N_BRANCH = 3
BRANCH_W = 512
D_FF = 4 * D_MODEL

LANES = 128
ROW_TILE = 256
CHUNK = 64
SCAN_BATCH = 4
NEUMANN_SINGLE_PASS_LEVELS = 1
VMEM_LIMIT = 56 * 1024 * 1024

_C_GQ = 0
_C_GK = _C_GQ + GQA_HEADS * GQA_HEAD_DIM
_C_GV = _C_GK + GQA_KV_HEADS * GQA_HEAD_DIM
_C_RW = _C_GV + GQA_KV_HEADS * GQA_HEAD_DIM
_C_QD = _C_RW + RWKV_IN
_C_KVD = _C_QD + Q_LORA
_C_KR = _C_KVD + KV_LORA


def _cparams(n_axes):
    return pltpu.CompilerParams(dimension_semantics=("arbitrary",) * n_axes,
                                vmem_limit_bytes=VMEM_LIMIT)


def _dot(a, b):
    return jnp.dot(a, b, preferred_element_type=F32)


def _dot_nt(a, b):
    return lax.dot_general(a, b, (((1,), (1,)), ((), ())), preferred_element_type=F32)


def _dot_tn(a, b):
    return lax.dot_general(a, b, (((0,), (0,)), ((), ())), preferred_element_type=F32)


def _split(x):
    hi = x.astype(BF16)
    lo = (x - hi.astype(F32)).astype(BF16)
    return hi, lo


def _dot3(a, b, dot=_dot):
    m = a[0].shape[0]
    both = dot(jnp.concatenate([a[0], a[1]], axis=0), b[0])
    return both[0:m] + both[m:] + dot(a[0], b[1])


def _dot3_tn(a, b):
    m = a[0].shape[1]
    both = _dot_tn(jnp.concatenate([a[0], a[1]], axis=1), b[0])
    return both[0:m] + both[m:] + _dot_tn(a[0], b[1])


def _group_sum(x, bd):
    outs = []
    for g in range(x.shape[1] // LANES):
        hi, lo = _split(x[:, g * LANES:(g + 1) * LANES])
        outs.append(_dot(hi, bd) + _dot(lo, bd))
    return outs[0] if len(outs) == 1 else jnp.concatenate(outs, axis=1)


def _rope(x, cos, sin, half):
    lane = lax.broadcasted_iota(jnp.int32, x.shape, 1)
    up = pltpu.roll(x, LANES - half, axis=1)
    dn = pltpu.roll(x, half, axis=1)
    sw = jnp.where((lane % (2 * half)) < half, up, dn)
    return x * cos + sw * sin


def _rms(x, gain):
    ms = jnp.mean(x * x, axis=-1, keepdims=True)
    return x * lax.rsqrt(ms + NORM_EPS) * gain


def _mod_kernel(c_ref, w_ref, b_ref, o_ref):
    c = c_ref[...]
    s = (c * jax.nn.sigmoid(c)).astype(BF16)
    o_ref[...] = _dot(s, w_ref[...].astype(BF16)) + b_ref[...]


def _mod_call(c_all, w_mod, b_mod):
    n_layers, d, n = w_mod.shape
    bn = 1536
    return pl.pallas_call(
        _mod_kernel,
        out_shape=jax.ShapeDtypeStruct((n_layers, 8, n), F32),
        grid=(n_layers, n // bn),
        in_specs=[
            pl.BlockSpec((8, d), lambda l, j: (0, 0)),
            pl.BlockSpec((None, d, bn), lambda l, j: (l, 0, j)),
            pl.BlockSpec((None, 1, bn), lambda l, j: (l, 0, j)),
        ],
        out_specs=pl.BlockSpec((None, 8, bn), lambda l, j: (l, 0, j)),
        compiler_params=_cparams(2),
        name="adaln_mod",
    )(c_all, w_mod, b_mod.reshape(n_layers, 1, n))


def _stream_kernel(body, n_src, n_tiles, n_ctx_tiles):
    def kern(*refs):
        if n_src == 1:
            x = refs[0][...]
        else:
            is_ctx = pl.program_id(0) % n_tiles < n_ctx_tiles
            x = jnp.where(is_ctx, refs[0][...], refs[1][...])
        body(x, *refs[n_src:])
    return kern


def _stream_specs(xs, n_tiles, n_ctx_tiles):
    if not isinstance(xs, tuple):
        return (xs,), [_rows(D_MODEL)]
    n_lat = n_tiles - n_ctx_tiles
    ctx_spec = pl.BlockSpec((ROW_TILE, D_MODEL), lambda i: (
        (i // n_tiles) * n_ctx_tiles + jnp.minimum(i % n_tiles, n_ctx_tiles - 1), 0))
    lat_spec = pl.BlockSpec((ROW_TILE, D_MODEL), lambda i: (
        (i // n_tiles) * n_lat + jnp.maximum(i % n_tiles - n_ctx_tiles, 0), 0))
    return xs, [ctx_spec, lat_spec]


def _inproj_body(x, g1_ref, sh_ref, sc_ref, w_ref, wkr_ref, gain_ref, bd_ref,
                 gc_ref, gs_ref, mc_ref, ms_ref,
                 qn_ref, qup_ref, kvn_ref, kvk_ref, kvv_ref,
                 gq_ref, gk_ref, gv_ref, ru_ref, mq_ref, mk_ref, mv_ref):
    h = (_rms(x, g1_ref[...]) * (1.0 + sc_ref[...]) + sh_ref[...]).astype(BF16)
    p = _dot(h, w_ref[...])

    bd = bd_ref[...]
    gc, gs = gc_ref[...], gs_ref[...]
    n_q_slabs = GQA_HEADS * GQA_HEAD_DIM // LANES
    n_k_slabs = GQA_KV_HEADS * GQA_HEAD_DIM // LANES
    for g in range(n_q_slabs + n_k_slabs):
        slab = p[:, g * LANES:(g + 1) * LANES]
        ms = _group_sum(slab * slab, bd) * (1.0 / GQA_HEAD_DIM)
        y = slab * lax.rsqrt(ms + NORM_EPS) * gain_ref[:, g * LANES:(g + 1) * LANES]
        y = _rope(y, gc, gs, GQA_HEAD_DIM // 4)
        if g < n_q_slabs:
            gq_ref[:, g * LANES:(g + 1) * LANES] = (y * (GQA_SCALE * LOG2_E)).astype(BF16)
        else:
            gk_ref[:, (g - n_q_slabs) * LANES:(g - n_q_slabs + 1) * LANES] = y.astype(BF16)
    gv_ref[...] = p[:, _C_GV:_C_RW].astype(BF16).T

    ru_ref[...] = p[:, _C_RW:_C_QD]

    mc, msn = mc_ref[...], ms_ref[...]
    qd = _rms(p[:, _C_QD:_C_KVD], qn_ref[...])
    q = _dot(qd.astype(BF16), qup_ref[...])
    kvd = _rms(p[:, _C_KVD:_C_KR], kvn_ref[...]).astype(BF16)
    kn = _dot(kvd, kvk_ref[...])
    kr = _rope(_dot(h, wkr_ref[...]), mc, msn, QK_ROPE // 4)
    for hd in range(MLA_HEADS):
        sl = slice(hd * MLA_SLOT, (hd + 1) * MLA_SLOT)
        mq_ref[:, sl] = (_rope(q[:, sl], mc, msn, QK_ROPE // 4) * (MLA_SCALE * LOG2_E)).astype(BF16)
        mk_ref[:, sl] = (kn[:, sl] + kr).astype(BF16)
    mv_ref[...] = _dot(kvd, kvv_ref[...]).astype(BF16).T


def _row_mod_spec(which, n_tiles, n_ctx_tiles, ctx_row):
    def index(i):
        b, t = i // n_tiles, i % n_tiles
        return (jnp.where(t < n_ctx_tiles, ctx_row, b), which, 0, 0)
    return pl.BlockSpec((None, None, 1, D_MODEL), index)


def _full(shape):
    return pl.BlockSpec(shape, lambda i: (0,) * len(shape))


def _rows(width):
    return pl.BlockSpec((ROW_TILE, width), lambda i: (i, 0))


def _inproj_call(xs, mod, g1, w_main, w_kr, gains, bd, tabs, qn, qup, kvn, kvk, kvv, geom):
    n_rows, n_tiles, n_ctx_tiles, ctx_row = geom
    tab_spec = pl.BlockSpec((ROW_TILE, LANES), lambda i: (i % n_tiles, 0))
    widths = (GQA_HEADS * GQA_HEAD_DIM, GQA_KV_HEADS * GQA_HEAD_DIM, GQA_KV_HEADS * GQA_HEAD_DIM,
              RWKV_IN, MLA_HEADS * MLA_SLOT, MLA_HEADS * MLA_SLOT, MLA_HEADS * V_HEAD)
    dtypes = (BF16, BF16, BF16, F32, BF16, BF16, BF16)
    transposed = (False, False, True, False, False, False, True)
    n_batch = n_rows // (n_tiles * ROW_TILE)
    out_shapes = [jax.ShapeDtypeStruct((n_batch, w, n_tiles * ROW_TILE) if tr else (n_rows, w), dt)
                  for w, dt, tr in zip(widths, dtypes, transposed)]
    out_specs = [pl.BlockSpec((None, w, ROW_TILE), lambda i: (i // n_tiles, 0, i % n_tiles)) if tr else _rows(w)
                 for w, tr in zip(widths, transposed)]
    srcs, src_specs = _stream_specs(xs, n_tiles, n_ctx_tiles)
    return pl.pallas_call(
        _stream_kernel(_inproj_body, len(srcs), n_tiles, n_ctx_tiles),
        out_shape=out_shapes,
        grid=(n_rows // ROW_TILE,),
        in_specs=[
            *src_specs, _full((1, D_MODEL)),
            _row_mod_spec(0, n_tiles, n_ctx_tiles, ctx_row),
            _row_mod_spec(1, n_tiles, n_ctx_tiles, ctx_row),
            _full(w_main.shape), _full(w_kr.shape), _full(gains.shape), _full(bd.shape),
            tab_spec, tab_spec, tab_spec, tab_spec,
            _full(qn.shape), _full(qup.shape), _full(kvn.shape), _full(kvk.shape), _full(kvv.shape),
        ],
        out_specs=out_specs,
        compiler_params=_cparams(1),
        name="in_proj",
    )(*srcs, g1, mod, mod, w_main, w_kr, gains, bd, *tabs, qn, qup, kvn, kvk, kvv)


def _attn_kernel(q_ref, k_ref, vt_ref, o_ref, s_ref, *, heads, group, dq, dv, n_ctx_tiles, ctx_len, s_tot):
    t = pl.program_id(1)
    sub = 8
    width = LANES // dv
    assert heads % width == 0

    def run(nk):
        blocks = [(lo, min(lo + ATTN_KEY_BLOCK, nk)) for lo in range(0, nk, ATTN_KEY_BLOCK)]

        def score_block(hq, lo, hi, m8):
            hk = hq // group
            sj = _dot_nt(k_ref[lo:hi, hk * dq:(hk + 1) * dq], q_ref[:, hq * dq:(hq + 1) * dq])
            s_ref[hq % (2 * width), lo:hi, :] = sj
            mj = jnp.max(sj.reshape((hi - lo) // sub, sub, ROW_TILE), axis=0)
            return mj if m8 is None else jnp.maximum(m8, mj)

        def prob_block(hq, lo, hi, m, l8, acc):
            hk = hq // group
            e = jnp.exp2(s_ref[hq % (2 * width), lo:hi, :] - m)
            lj = jnp.sum(e.reshape((hi - lo) // sub, sub, ROW_TILE), axis=0)
            pj = _dot(vt_ref[hk * dv:(hk + 1) * dv, lo:hi], e.astype(BF16))
            return (lj if l8 is None else l8 + lj), (pj if acc is None else acc + pj)

        m8 = [None] * width
        for lo, hi in blocks:
            for w in range(width):
                m8[w] = score_block(w, lo, hi, m8[w])
        for h0 in range(0, heads, width):
            m = [jnp.max(m8[w], axis=0, keepdims=True) for w in range(width)]
            m8, l8, acc = [None] * width, [None] * width, [None] * width
            for lo, hi in blocks:
                for w in range(width):
                    if h0 + width + w < heads:
                        m8[w] = score_block(h0 + width + w, lo, hi, m8[w])
                    l8[w], acc[w] = prob_block(h0 + w, lo, hi, m[w], l8[w], acc[w])
            out_t = [acc[w] / jnp.sum(l8[w], axis=0, keepdims=True) for w in range(width)]
            o_ref[:, h0 * dv:(h0 + width) * dv] = jnp.concatenate(out_t, axis=0).T.astype(o_ref.dtype)

    @pl.when(t < n_ctx_tiles)
    def _():
        run(ctx_len)

    @pl.when(t >= n_ctx_tiles)
    def _():
        run(s_tot)


def _attn_call(q, k, vt, *, heads, group, dq, dv, batch, s_tot, ctx_len, name):
    n_tiles = s_tot // ROW_TILE
    assert ctx_len % LANES == 0 and s_tot % LANES == 0
    kern = functools.partial(_attn_kernel, heads=heads, group=group, dq=dq, dv=dv,
                             n_ctx_tiles=ctx_len // ROW_TILE, ctx_len=ctx_len, s_tot=s_tot)
    return pl.pallas_call(
        kern,
        out_shape=jax.ShapeDtypeStruct((batch * s_tot, heads * dv), BF16),
        grid=(batch, n_tiles),
        in_specs=[
            pl.BlockSpec((ROW_TILE, q.shape[1]), lambda b, t: (b * n_tiles + t, 0)),
            pl.BlockSpec((s_tot, k.shape[1]), lambda b, t: (b, 0)),
            pl.BlockSpec((None, vt.shape[1], s_tot), lambda b, t: (b, 0, 0)),
        ],
        out_specs=pl.BlockSpec((ROW_TILE, heads * dv), lambda b, t: (b * n_tiles + t, 0)),
        scratch_shapes=[pltpu.VMEM((2 * (LANES // dv), s_tot, ROW_TILE), F32)],
        compiler_params=_cparams(2),
        name=name,
    )(q, k, vt)


def _rwkv_prep_kernel(u_ref, prev_ref, next_ref, mu_ref, kk_ref, ka_ref, rk_ref, w0_ref, w2_ref,
                      a0_ref, a2_ref, g2_ref, bd_ref, tri_ref,
                      v_ref, bonus_ref, gate_ref, lhs0_ref, rhs0_ref, lhs1_ref, rhs1_ref, pc_ref,
                      *, n_tiles, n_ctx_tiles):
    t = pl.program_id(0) % n_tiles
    u = u_ref[...]
    rows = u.shape[0]
    sub = 8
    row = lax.broadcasted_iota(jnp.int32, (sub, u.shape[1]), 0)
    seq_start = jnp.logical_or(t == 0, t == n_ctx_tiles)
    seq_end = jnp.logical_or(t == n_ctx_tiles - 1, t == n_tiles - 1)
    halo_prev = prev_ref[sub - 1:sub, :] * jnp.where(seq_start, 0.0, 1.0)
    halo_next = next_ref[0:1, :] * jnp.where(seq_end, 0.0, 1.0)
    prev = pltpu.roll(u, 1, axis=0)
    nxt = pltpu.roll(u, rows - 1, axis=0)
    prev = jnp.concatenate([jnp.where(row == 0, halo_prev, prev[0:sub]), prev[sub:]], axis=0)
    nxt = jnp.concatenate([nxt[0:rows - sub], jnp.where(row == sub - 1, halo_next, nxt[rows - sub:])], axis=0)
    mu_p, mu_n = mu_ref[0:1, :], mu_ref[1:2, :]
    us = u * (1.0 - mu_p - mu_n) + mu_p * prev + mu_n * nxt

    c = RWKV_DIM
    r, k, v = us[:, 0:c], us[:, c:2 * c], us[:, 2 * c:3 * c]
    wd = jnp.tanh(us[:, 3 * c:3 * c + LANES]).astype(BF16)
    ad = us[:, 3 * c + LANES:3 * c + 2 * LANES].astype(BF16)
    gd = us[:, 3 * c + 2 * LANES:3 * c + 3 * LANES]
    bd = bd_ref[...]

    kk = k * kk_ref[...]
    kk = kk * lax.rsqrt(jnp.maximum(_group_sum(kk * kk, bd), 1e-24))

    v_ref[...] = v.astype(BF16)
    gate_ref[...] = _dot(jax.nn.sigmoid(gd).astype(BF16), g2_ref[...])

    outs = ((lhs0_ref, rhs0_ref), (lhs1_ref, rhs1_ref))
    n_chunks = rows // CHUNK
    bonus = None
    for d in range(2):
        z = w0_ref[d:d + 1, :] + _dot(wd, w2_ref[d])
        nz = -z
        softplus = jnp.maximum(nz, 0.0) + jnp.log(1.0 + jnp.exp(-jnp.abs(nz)))
        lw = -jnp.exp(-softplus - 0.5)
        a = jax.nn.sigmoid(a0_ref[d:d + 1, :] + _dot(ad, a2_ref[d]))
        k_d = k * (1.0 + (a - 1.0) * ka_ref[...])
        b_d = _group_sum(r * k_d * rk_ref[...], bd) * v
        bonus = b_d if bonus is None else bonus + b_d

        tri = tri_ref[d]
        hi = lw.astype(BF16)
        rem = lw - hi.astype(F32)
        mid = rem.astype(BF16)
        lo = (rem - mid.astype(F32)).astype(BF16)
        cum = _dot(tri, hi) + (_dot(tri, mid) + _dot(tri, lo))
        e_neg = jnp.exp(-cum)
        lhs_ref, rhs_ref = outs[d]
        rt = (r * jnp.exp(cum)).astype(BF16)
        at = (-kk * jnp.exp(cum - lw)).astype(BF16)
        kt = (k_d * e_neg).astype(BF16)
        bt = (kk * a * e_neg).astype(BF16)
        for ci in range(n_chunks):
            rs = slice(ci * CHUNK, (ci + 1) * CHUNK)
            lhs_ref[ci, 0:CHUNK, :] = at[rs]
            lhs_ref[ci, CHUNK:, :] = rt[rs]
            rhs_ref[ci, 0:CHUNK, :] = bt[rs]
            rhs_ref[ci, CHUNK:, :] = kt[rs]
        last = CHUNK - 1 if d == 0 else 0
        tot = jnp.concatenate([cum[ci * CHUNK + last:ci * CHUNK + last + 1, :] for ci in range(n_chunks)], axis=0)
        pc_ref[d * n_chunks:(d + 1) * n_chunks, :] = jnp.exp(tot)
    bonus_ref[...] = bonus


def _rwkv_prep_call(ru, mu, kk, ka, rk, w0, w2p, a0, a2p, g2, bd, tri, geom):
    n_rows, n_tiles, n_ctx_tiles, _ = geom
    grid = n_rows // ROW_TILE
    g8 = ROW_TILE // 8
    last8 = n_rows // 8 - 1
    kern = functools.partial(_rwkv_prep_kernel, n_tiles=n_tiles, n_ctx_tiles=n_ctx_tiles)
    c = RWKV_DIM
    n_pc = 2 * (ROW_TILE // CHUNK)
    return pl.pallas_call(
        kern,
        out_shape=[jax.ShapeDtypeStruct((n_rows, c), BF16)] + [jax.ShapeDtypeStruct((n_rows, c), F32)] * 2
                  + [jax.ShapeDtypeStruct((n_rows // CHUNK, 2 * CHUNK, c), BF16)] * 4
                  + [jax.ShapeDtypeStruct((grid * n_pc, c), F32)],
        grid=(grid,),
        in_specs=[
            _rows(RWKV_IN),
            pl.BlockSpec((8, RWKV_IN), lambda i: (jnp.maximum(i * g8 - 1, 0), 0)),
            pl.BlockSpec((8, RWKV_IN), lambda i: (jnp.minimum((i + 1) * g8, last8), 0)),
            _full(mu.shape), _full(kk.shape), _full(ka.shape), _full(rk.shape),
            _full(w0.shape), _full(w2p.shape), _full(a0.shape), _full(a2p.shape),
            _full(g2.shape), _full(bd.shape), _full(tri.shape),
        ],
        out_specs=[_rows(c)] * 3
                  + [pl.BlockSpec((ROW_TILE // CHUNK, 2 * CHUNK, c), lambda i: (i, 0, 0))] * 4
                  + [pl.BlockSpec((n_pc, c), lambda i: (i, 0))],
        compiler_params=_cparams(1),
        name="rwkv_prep",
    )(ru, ru, ru, mu, kk, ka, rk, w0, w2p, a0, a2p, g2, bd, tri)


def _rwkv_scan_kernel(*refs, nb):
    ins, (y0_ref, y1_ref, s_ref) = refs[:8], refs[8:]
    y_refs = (y0_ref, y1_ref)

    @pl.when(pl.program_id(1) == 0)
    def _():
        s_ref[...] = jnp.zeros_like(s_ref)

    n = RWKV_HEAD
    c = CHUNK
    ri = lax.broadcasted_iota(jnp.int32, (2 * c, 2 * c), 0)
    ci = lax.broadcasted_iota(jnp.int32, (2 * c, 2 * c), 1)
    tq, ts = ri % c, ci % c
    masks = []
    for d in range(2):
        earlier = (ts < tq) if d == 0 else (ts > tq)
        masks.append(jnp.where(jnp.logical_or(earlier, jnp.logical_and(ri >= c, ts == tq)), 1.0, 0.0))

    chains = [(b, d, hd) for b in range(nb) for d in range(2) for hd in range(RWKV_HEADS)]
    bf = lambda t: t.astype(BF16)

    def head(arr, hd):
        return arr[:, hd * n:(hd + 1) * n]

    lhs_h = [head(ins[4 * d][b], hd) for b, d, hd in chains]
    rhs_h = [head(ins[4 * d + 1][b], hd) for b, d, hd in chains]
    v_h = [head(ins[4 * d + 2][b], hd) for b, d, hd in chains]
    pc_h = [head(ins[4 * d + 3][b], hd) for b, d, hd in chains]
    s_h = [head(s_ref[b, d], hd) for b, d, hd in chains]
    gram = [_dot_nt(l_, r_) * masks[d] for (_, d, _), l_, r_ in zip(chains, lhs_h, rhs_h)]
    ls = [_dot_nt(l_, bf(si)) for l_, si in zip(lhs_h, s_h)]
    zeros = jnp.zeros((c, n), BF16)
    xin = [a[0:c] + _dot(bf(g[0:c]), jnp.concatenate([zeros, vi], axis=0))
           for a, g, vi in zip(ls, gram, v_h)]
    right = lax.broadcasted_iota(jnp.int32, (c, 2 * n), 1) >= n
    rj = [jnp.concatenate([g[0:c, 0:n], xi], axis=1) for g, xi in zip(gram, xin)]
    n_levels = c.bit_length() - 1
    for level in range(n_levels):
        nxt = []
        for r_ in rj:
            hi = bf(r_)
            if level < n_levels - NEUMANN_SINGLE_PASS_LEVELS:
                lo = bf(r_ - hi.astype(F32))
                both = _dot(jnp.concatenate([hi[:, 0:n], lo[:, 0:n]], axis=0), hi)
                p = both[0:c] + both[c:] + _dot(hi[:, 0:n], lo)
            else:
                p = _dot(hi[:, 0:n], hi)
            nxt.append(p + jnp.where(right, r_, 0.0))
        rj = nxt
    uv = [jnp.concatenate([bf(r_[:, n:]), vi], axis=0) for r_, vi in zip(rj, v_h)]
    y = [a[c:] + _dot(bf(g[c:]), uvi) for a, g, uvi in zip(ls, gram, uv)]
    s_new = [(si + _dot_tn(uvi, r_)) * pci for si, uvi, r_, pci in zip(s_h, uv, rhs_h, pc_h)]
    per = RWKV_HEADS
    for b in range(nb):
        for d in range(2):
            first = (b * 2 + d) * per
            y_refs[d][b] = jnp.concatenate(y[first:first + per], axis=1)
            s_ref[b, d] = jnp.concatenate(s_new[first:first + per], axis=1)


def _rwkv_scan_call(prep, batch, s_tot, ctx_len):
    v, _, _, lhs0, rhs0, lhs1, rhs1, pc = prep
    n_ch = s_tot // CHUNK
    n_ctx_ch = ctx_len // CHUNK
    per_tile = ROW_TILE // CHUNK
    c = RWKV_DIM
    nb = SCAN_BATCH if batch % SCAN_BATCH == 0 else 1

    def chunk_of(d, i):
        if d == 0:
            return i
        return jnp.where(i < n_ctx_ch, n_ctx_ch - 1 - i, n_ctx_ch + n_ch - 1 - i)

    def stacked_spec(d):
        return pl.BlockSpec((nb, None, 2 * CHUNK, c), lambda b, i: (b, chunk_of(d, i), 0, 0))

    def row_spec(d):
        return pl.BlockSpec((nb, CHUNK, c), lambda b, i: (b, chunk_of(d, i), 0))

    def pc_spec(d):
        def index(b, i):
            ch = chunk_of(d, i)
            return (b, (ch // per_tile) * 2 * per_tile + d * per_tile + ch % per_tile, 0, 0)
        return pl.BlockSpec((nb, None, 1, c), index)

    v3 = v.reshape(batch, s_tot, c)
    pc4 = pc.reshape(batch, pc.shape[0] // batch, 1, c)
    ins, specs = [], []
    for d, (lhs, rhs) in enumerate(((lhs0, rhs0), (lhs1, rhs1))):
        ins += [lhs.reshape(batch, n_ch, 2 * CHUNK, c), rhs.reshape(batch, n_ch, 2 * CHUNK, c), v3, pc4]
        specs += [stacked_spec(d), stacked_spec(d), row_spec(d), pc_spec(d)]
    y0, y1 = pl.pallas_call(
        functools.partial(_rwkv_scan_kernel, nb=nb),
        out_shape=[jax.ShapeDtypeStruct((batch, s_tot, c), F32)] * 2,
        grid=(batch // nb, n_ch),
        in_specs=specs,
        out_specs=[row_spec(0), row_spec(1)],
        scratch_shapes=[pltpu.VMEM((nb, 2, RWKV_HEAD, c), F32)],
        compiler_params=_cparams(2),
        name="rwkv_scan",
    )(*ins)
    return y0.reshape(batch * s_tot, c), y1.reshape(batch * s_tot, c)


def _merge_body(x, g1_ref, sh_ref, sc_ref, gt_ref, a_ref, y0_ref, y1_ref, bonus_ref, rg_ref, m_ref,
                lnw_ref, lnb_ref, bd_ref, wg_ref, wb_ref, wo_ref, o_ref):
    h = (_rms(x, g1_ref[...]) * (1.0 + sc_ref[...]) + sh_ref[...]).astype(BF16)

    bd = bd_ref[...]
    y = y0_ref[...] + y1_ref[...]
    mu = _group_sum(y, bd) * (1.0 / RWKV_HEAD)
    yc = y - mu
    var = _group_sum(yc * yc, bd) * (1.0 / RWKV_HEAD)
    yn = yc * lax.rsqrt(var + LNX_EPS) * lnw_ref[...] + lnb_ref[...]
    r_out = ((yn + bonus_ref[...]) * rg_ref[...]).astype(BF16)

    d = D_MODEL
    mixed = None
    for i, yb in enumerate((a_ref[...], r_out, m_ref[...])):
        gate = jax.nn.sigmoid(_dot(h, wg_ref[:, i * d:(i + 1) * d]))
        term = gate * _dot(yb, wb_ref[i])
        mixed = term if mixed is None else mixed + term
    o_ref[...] = x + gt_ref[...] * _dot(mixed.astype(BF16), wo_ref[...])


def _merge_call(xs, mod, g1, a, y0, y1, bonus, rg, m, lnw, lnb, bd, wg, wb, wo, geom, latent_only):
    n_rows, n_tiles, n_ctx_tiles, ctx_row = geom
    c = RWKV_DIM
    if latent_only:
        n_lat = n_tiles - n_ctx_tiles
        n_out_tiles = (n_rows // ROW_TILE) // n_tiles * n_lat
        src_tile = lambda i: (i // n_lat) * n_tiles + n_ctx_tiles + i % n_lat
        ms = lambda which: _row_mod_spec(which, n_lat, 0, ctx_row)
    else:
        n_out_tiles = n_rows // ROW_TILE
        src_tile = lambda i: i
        ms = lambda which: _row_mod_spec(which, n_tiles, n_ctx_tiles, ctx_row)
    rows_in = lambda width: pl.BlockSpec((ROW_TILE, width), lambda i: (src_tile(i), 0))
    if isinstance(xs, tuple):
        assert not latent_only
        srcs, src_specs = _stream_specs(xs, n_tiles, n_ctx_tiles)
    else:
        srcs, src_specs = (xs,), [rows_in(D_MODEL)]
    return pl.pallas_call(
        _stream_kernel(_merge_body, len(srcs), n_tiles, n_ctx_tiles),
        out_shape=jax.ShapeDtypeStruct((n_out_tiles * ROW_TILE, D_MODEL), F32),
        grid=(n_out_tiles,),
        in_specs=[
            *src_specs, _full((1, D_MODEL)), ms(0), ms(1), ms(2),
            rows_in(c), rows_in(c), rows_in(c), rows_in(c), rows_in(c), rows_in(c),
            _full(lnw.shape), _full(lnb.shape), _full(bd.shape),
            _full(wg.shape), _full(wb.shape), _full(wo.shape),
        ],
        out_specs=_rows(D_MODEL),
        compiler_params=_cparams(1),
        name="merge",
    )(*srcs, g1, mod, mod, mod, a, y0, y1, bonus, rg, m, lnw, lnb, bd, wg, wb, wo)


def _mlp_kernel(x_ref, g2_ref, sh_ref, sc_ref, gt_ref, w1_ref, w2_ref, gf_ref, o_ref, *, final_norm):
    x = x_ref[...]
    h = (_rms(x, g2_ref[...]) * (1.0 + sc_ref[...]) + sh_ref[...]).astype(BF16)
    acc = None
    blk = D_MODEL
    for j in range(D_FF // blk):
        a = jnp.maximum(_dot(h, w1_ref[:, j * blk:(j + 1) * blk]), 0.0)
        part = _dot((a * a).astype(BF16), w2_ref[j * blk:(j + 1) * blk, :])
        acc = part if acc is None else acc + part
    out = x + gt_ref[...] * acc
    if final_norm:
        out = _rms(out, gf_ref[...])
    o_ref[...] = out


def _mlp_call(xs, mod, g2, w1, w2, gf, geom, final_norm):
    n_rows, n_tiles, n_ctx_tiles, ctx_row = geom
    ms = lambda which: _row_mod_spec(which, n_tiles, n_ctx_tiles, ctx_row)
    return pl.pallas_call(
        functools.partial(_mlp_kernel, final_norm=final_norm),
        out_shape=jax.ShapeDtypeStruct((n_rows, D_MODEL), F32),
        grid=(n_rows // ROW_TILE,),
        in_specs=[
            _rows(D_MODEL), _full((1, D_MODEL)), ms(3), ms(4), ms(5),
            _full(w1.shape), _full(w2.shape), _full((1, D_MODEL)),
        ],
        out_specs=_rows(D_MODEL),
        compiler_params=_cparams(1),
        name="mlp",
    )(xs, g2, mod, mod, mod, w1, w2, gf)


def _rope_tables(ctx_len, seq):
    pos = jnp.arange(seq, dtype=jnp.int32)
    rows, cols = pos // GRID_W, pos % GRID_W

    def cs(p, n):
        inv = ROPE_THETA ** (-jnp.arange(n, dtype=F32) / n)
        ang = p.astype(F32)[:, None] * inv[None, :]
        return jnp.cos(ang), jnp.sin(ang)

    def axial(n):
        cr, sr = cs(rows, n)
        cc, sc = cs(cols, n)
        return jnp.concatenate([cr, cr, cc, cc], -1), jnp.concatenate([-sr, sr, -sc, sc], -1)

    def with_ctx(cos, sin):
        w = cos.shape[1]
        return (jnp.concatenate([jnp.ones((ctx_len, w), F32), cos], 0),
                jnp.concatenate([jnp.zeros((ctx_len, w), F32), sin], 0))

    gc, gs = axial(GQA_HEAD_DIM // 4)
    gc, gs = jnp.tile(gc, (1, 2)), jnp.tile(gs, (1, 2))
    mc, ms = axial(QK_ROPE // 4)
    pad = MLA_SLOT - QK_NOPE - QK_ROPE
    mc = jnp.concatenate([jnp.ones((seq, QK_NOPE), F32), mc, jnp.ones((seq, pad), F32)], -1)
    ms = jnp.concatenate([jnp.zeros((seq, QK_NOPE), F32), ms, jnp.zeros((seq, pad), F32)], -1)
    return with_ctx(gc, gs) + with_ctx(mc, ms)


def _block_ones(n, blk):
    i = jnp.arange(n)
    return (i[:, None] // blk == i[None, :] // blk)


def _chunk_tri():
    i = jnp.arange(ROW_TILE)
    same = _block_ones(ROW_TILE, CHUNK)
    fwd = jnp.logical_and(same, i[None, :] <= i[:, None])
    bwd = jnp.logical_and(same, i[None, :] >= i[:, None])
    return jnp.stack([fwd, bwd]).astype(BF16)


def _pack_layer(l, w_in, gqa_q_gain, gqa_k_gain, mla_q_up, mla_kv_up, rwkv_w2, rwkv_a2):
    w = w_in[l]
    o_kr = _C_KR
    o_gate = o_kr + QK_ROPE
    w_main = w[:, 0:o_kr]
    w_kr = jnp.pad(w[:, o_kr:o_gate], ((0, 0), (QK_NOPE, MLA_SLOT - QK_NOPE - QK_ROPE)))
    w_gate = w[:, o_gate:]
    gains = jnp.concatenate([jnp.tile(gqa_q_gain[l], GQA_HEADS), jnp.tile(gqa_k_gain[l], GQA_KV_HEADS)])[None, :]

    qup = mla_q_up[l].reshape(Q_LORA, MLA_HEADS, QK_NOPE + QK_ROPE)
    qup = jnp.pad(qup, ((0, 0), (0, 0), (0, MLA_SLOT - QK_NOPE - QK_ROPE))).reshape(Q_LORA, MLA_HEADS * MLA_SLOT)
    kvu = mla_kv_up[l].reshape(KV_LORA, MLA_HEADS, QK_NOPE + V_HEAD)
    kvk = jnp.pad(kvu[:, :, :QK_NOPE], ((0, 0), (0, 0), (0, MLA_SLOT - QK_NOPE))).reshape(KV_LORA, MLA_HEADS * MLA_SLOT)
    kvv = kvu[:, :, QK_NOPE:].reshape(KV_LORA, MLA_HEADS * V_HEAD)

    def pad_dir(w2):
        lora = w2.shape[1]
        z = jnp.zeros_like(w2[0])
        return jnp.stack([jnp.concatenate([w2[0], z], 0), jnp.concatenate([z, w2[1]], 0)]).astype(BF16)

    return dict(w_main=w_main, w_kr=w_kr, w_gate=w_gate, gains=gains, qup=qup, kvk=kvk, kvv=kvv,
                w2p=pad_dir(rwkv_w2[l]), a2p=pad_dir(rwkv_a2[l]))


def kernel(x, c, ctx, c_ctx, w_mod, b_mod, g_norm1, g_norm2, w_in, gqa_q_gain, gqa_k_gain, rwkv_shift_mu, rwkv_w0, rwkv_w2, rwkv_a0, rwkv_a2, rwkv_g2, rwkv_k_k, rwkv_k_a, rwkv_r_k, rwkv_ln_w, rwkv_ln_b, mla_q_norm, mla_q_up, mla_kv_norm, mla_kv_up, w_branch, w_out, w_ff1, w_ff2, g_final):
    batch, seq, d = x.shape
    ctx_len = ctx.shape[1]
    depth = w_mod.shape[0]
    s_tot = ctx_len + seq
    assert d == D_MODEL and batch < 8
    assert ctx_len % ROW_TILE == 0 and seq % ROW_TILE == 0 and seq % GRID_W == 0
    n_rows = batch * s_tot
    geom = (n_rows, s_tot // ROW_TILE, ctx_len // ROW_TILE, batch)

    c_all = jnp.concatenate([c, c_ctx[None, :], jnp.zeros((8 - batch - 1, d), F32)], axis=0)
    mod_all = _mod_call(c_all, w_mod, b_mod).reshape(depth, 8, 6, 1, d)

    tabs = _rope_tables(ctx_len, seq)
    bd = _block_ones(LANES, RWKV_HEAD).astype(BF16)
    tri = _chunk_tri()
    row1 = lambda v: v.reshape(1, -1)

    w_in, mla_q_up, mla_kv_up, rwkv_g2, w_branch, w_out, w_ff1, w_ff2 = (
        w.astype(BF16) for w in (w_in, mla_q_up, mla_kv_up, rwkv_g2, w_branch, w_out, w_ff1, w_ff2))

    xs = (ctx.reshape(batch * ctx_len, d), x.reshape(batch * seq, d))
    if depth == 1:
        xs = jnp.concatenate([ctx, x], axis=1).reshape(n_rows, d)
    for l in range(depth):
        pk = _pack_layer(l, w_in, gqa_q_gain, gqa_k_gain, mla_q_up, mla_kv_up, rwkv_w2, rwkv_a2)
        mod = mod_all[l]
        g1 = row1(g_norm1[l])
        gq, gk, gv, ru, mq, mk, mv = _inproj_call(
            xs, mod, g1, pk["w_main"], pk["w_kr"], pk["gains"], bd, tabs,
            row1(mla_q_norm[l]), pk["qup"], row1(mla_kv_norm[l]), pk["kvk"], pk["kvv"], geom)
        a_out = _attn_call(gq, gk, gv, heads=GQA_HEADS, group=GQA_HEADS // GQA_KV_HEADS, dq=GQA_HEAD_DIM,
                           dv=GQA_HEAD_DIM, batch=batch, s_tot=s_tot, ctx_len=ctx_len, name="gqa_attn")
        m_out = _attn_call(mq, mk, mv, heads=MLA_HEADS, group=1, dq=MLA_SLOT, dv=V_HEAD,
                           batch=batch, s_tot=s_tot, ctx_len=ctx_len, name="mla_attn")
        prep = _rwkv_prep_call(ru, rwkv_shift_mu[l], row1(rwkv_k_k[l]), row1(rwkv_k_a[l]), row1(rwkv_r_k[l]),
                               rwkv_w0[l], pk["w2p"], rwkv_a0[l], pk["a2p"], rwkv_g2[l], bd, tri, geom)
        y0, y1 = _rwkv_scan_call(prep, batch, s_tot, ctx_len)
        last = l == depth - 1
        xs = _merge_call(xs, mod, g1, a_out, y0, y1, prep[1], prep[2], m_out,
                         row1(rwkv_ln_w[l]), row1(rwkv_ln_b[l]), bd, pk["w_gate"],
                         w_branch[l], w_out[l], geom, latent_only=last)
        mlp_geom = (batch * seq, seq // ROW_TILE, 0, batch) if last else geom
        xs = _mlp_call(xs, mod, row1(g_norm2[l]), w_ff1[l], w_ff2[l],
                       row1(g_final), mlp_geom, final_norm=last)
    return xs.reshape(batch, seq, d)
```
